```python
import math
import jax, jax.numpy as jnp
from jax import lax
import numpy as np

D_MODEL = 1024
BATCH = 2
SEQ = 8192
DEPTH = 4
DEC_BATCH = 32
DEC_SEQ = 1
PAST_LEN = 8192
PAGE_SIZE = 128

N_MIXERS = 3
N_A = (DEPTH + 2) // 3
N_B = (DEPTH + 1) // 3
N_C = DEPTH // 3

CONV_A_WIDTH = 31
CONV_B_WIDTH = 3
D_FF = ((8 * D_MODEL + 3 * 256 - 1) // (3 * 256)) * 256

HEAD_DIM = 64
N_HEADS = D_MODEL // HEAD_DIM
N_KV = 4
GROUP = N_HEADS // N_KV
ROT_DIM = HEAD_DIM // 4
ROPE_THETA = 500000.0
CMP_BLOCK = 32
CMP_STRIDE = 16
SEL_BLOCK = 64
N_SELECT = 16
WINDOW = 512
Q_BLOCK = 128
EPS = 1e-6
FORCE_SCORE = 1e9
NSA_IN = N_HEADS * HEAD_DIM + 6 * N_KV * HEAD_DIM + 3 * N_HEADS

kernel_name = "hybrid_conformer_shortconv_nsa_decode_step"


def rmsnorm(x, g):
    xf = x.astype(jnp.float32)
    r = xf * lax.rsqrt(jnp.mean(xf * xf, axis=-1, keepdims=True) + EPS)
    return (r * g.astype(jnp.float32)).astype(x.dtype)


def layernorm(x, g, b):
    xf = x.astype(jnp.float32)
    mu = jnp.mean(xf, axis=-1, keepdims=True)
    var = jnp.mean(jnp.square(xf - mu), axis=-1, keepdims=True)
    y = (xf - mu) * lax.rsqrt(var + EPS) * g.astype(jnp.float32) + b.astype(jnp.float32)
    return y.astype(x.dtype)


def causal_dwconv(u, w):
    return lax.conv_general_dilated(
        u, w[:, None, :].astype(u.dtype), window_strides=(1,), padding='VALID',
        dimension_numbers=('NWC', 'WIO', 'NWC'), feature_group_count=u.shape[-1])


def apply_rope(x, pos):
    half = ROT_DIM // 2
    inv = jnp.exp(jnp.arange(half, dtype=jnp.float32) * (-2.0 * math.log(ROPE_THETA) / ROT_DIM))
    ang = pos.astype(jnp.float32)[:, None] * inv[None, :]
    shape = (1, pos.shape[0]) + (1,) * (x.ndim - 3) + (half,)
    cos = jnp.cos(ang).reshape(shape)
    sin = jnp.sin(ang).reshape(shape)
    xf = x.astype(jnp.float32)
    x1, x2 = xf[..., :half], xf[..., half:ROT_DIM]
    out = jnp.concatenate([x1 * cos - x2 * sin, x2 * cos + x1 * sin, xf[..., ROT_DIM:]], axis=-1)
    return out.astype(x.dtype)


def masked_softmax(s, mask):
    s = jnp.where(mask, s, -jnp.inf)
    m = jnp.max(s, axis=-1, keepdims=True)
    m = jnp.where(jnp.isfinite(m), m, 0.0)
    e = jnp.where(mask, jnp.exp(s - m), 0.0)
    return e / jnp.maximum(jnp.sum(e, axis=-1, keepdims=True), 1e-30)


def swiglu(x, w_gate, w_up, w_down):
    return (jax.nn.silu(x @ w_gate) * (x @ w_up)) @ w_down


def conformer_mix(xn, hist, w_in, b_in, w_dw, b_dw, ln_g, ln_b, w_out, b_out):
    h = xn @ w_in + b_in
    u = h[..., :D_MODEL] * jax.nn.sigmoid(h[..., D_MODEL:])
    ucat = jnp.concatenate([hist.astype(u.dtype), u], axis=1)
    c = causal_dwconv(ucat, w_dw) + b_dw
    c = layernorm(c, ln_g, ln_b)
    c = c * jax.nn.sigmoid(c)
    return c @ w_out + b_out, ucat[:, -(CONV_A_WIDTH - 1):]


def gated_conv_mix(xn, hist, w_in, w_dw, w_out):
    b_g, c_g, v = jnp.split(xn @ w_in, 3, axis=-1)
    ucat = jnp.concatenate([hist.astype(v.dtype), c_g * v], axis=1)
    z = b_g * causal_dwconv(ucat, w_dw)
    return z @ w_out, ucat[:, -(CONV_B_WIDTH - 1):]


def nsa_project(xn, pos, w_in, q_norm, k_norm):
    B, T, _ = xn.shape
    h = xn @ w_in
    nq = N_HEADS * HEAD_DIM
    nkv = 6 * N_KV * HEAD_DIM
    q = h[..., :nq].reshape(B, T, N_KV, GROUP, HEAD_DIM)
    kv = h[..., nq:nq + nkv].reshape(B, T, 3, 2, N_KV, HEAD_DIM)
    gates = h[..., nq + nkv:].reshape(B, T, N_KV, GROUP, 3)
    q = apply_rope(rmsnorm(q, q_norm), pos)
    k = apply_rope(rmsnorm(kv[:, :, :, 0], k_norm[:, None, :]), pos)
    kv = jnp.stack([k, kv[:, :, :, 1]], axis=3)
    paged = kv[:, :, :2].reshape(B, T, 4, N_KV, HEAD_DIM)
    win = kv[:, :, 2]
    return q, paged, win, gates


def compress(rows, w, pe):
    B, L = rows.shape[:2]
    n_cmp = -(-L // CMP_STRIDE)
    padded = jnp.pad(rows, ((0, 0), (0, (n_cmp + 1) * CMP_STRIDE - L), (0, 0), (0, 0)))
    chunks = padded.reshape(B, n_cmp + 1, CMP_STRIDE, N_KV, HEAD_DIM)
    blocks = jnp.concatenate([chunks[:, :-1], chunks[:, 1:]], axis=2)
    blocks = blocks + pe[None, None, :, None, :]
    return jnp.einsum('bnlhd,lde->bnhe', blocks, w.reshape(CMP_BLOCK, HEAD_DIM, HEAD_DIM))


def sel_blocks(rows):
    B, L = rows.shape[:2]
    n_sel = -(-L // SEL_BLOCK)
    padded = jnp.pad(rows, ((0, 0), (0, n_sel * SEL_BLOCK - L), (0, 0), (0, 0)))
    return padded.reshape(B, n_sel, SEL_BLOCK, N_KV, HEAD_DIM).transpose(0, 3, 1, 2, 4)


def nsa_core(q, q_pos, kc, vc, ksb, vsb, kw, vw, kw_pos, gates):
    B, Tq = q.shape[:2]
    scale = HEAD_DIM ** -0.5
    n_cmp, n_sel = kc.shape[1], ksb.shape[2]
    cmp_end = jnp.arange(n_cmp, dtype=jnp.int32) * CMP_STRIDE + (CMP_BLOCK - 1)
    mask_c = (cmp_end[None, :] <= q_pos[:, None])[None, :, None, None, :]
    s_c = jnp.einsum('bqhgd,bnhd->bqhgn', q, kc).astype(jnp.float32) * scale
    p_c = masked_softmax(s_c, mask_c)
    o_c = jnp.einsum('bqhgn,bnhd->bqhgd', p_c.astype(vc.dtype), vc)
    ci = jnp.arange(n_cmp, dtype=jnp.int32)[:, None] * CMP_STRIDE
    sj = jnp.arange(n_sel, dtype=jnp.int32)
    sel_start = sj * SEL_BLOCK
    overlap = ((ci < sel_start[None, :] + SEL_BLOCK) & (ci + CMP_BLOCK > sel_start[None, :])).astype(jnp.float32)
    imp = jnp.einsum('bqhn,nj->bqhj', jnp.sum(p_c, axis=3), overlap)
    cur = (q_pos // SEL_BLOCK)[:, None]
    forced = (sj[None, :] == 0) | (sj[None, :] == cur) | (sj[None, :] == cur - 1)
    imp = jnp.where(forced[None, :, None, :], FORCE_SCORE, imp)
    imp = jnp.where((sel_start[None, :] > q_pos[:, None])[None, :, None, :], -jnp.inf, imp)
    _, idx = lax.top_k(imp, min(N_SELECT, n_sel))
    bi = jnp.arange(B)[:, None, None, None]
    hi = jnp.arange(N_KV)[None, None, :, None]
    k_g = ksb[bi, hi, idx]
    v_g = vsb[bi, hi, idx]
    n_k = idx.shape[-1] * SEL_BLOCK
    tok = idx[..., None] * SEL_BLOCK + jnp.arange(SEL_BLOCK, dtype=jnp.int32)
    mask_s = (tok <= q_pos[None, :, None, None, None]).reshape(B, Tq, N_KV, 1, n_k)
    s_s = jnp.einsum('bqhgd,bqhkcd->bqhgkc', q, k_g).astype(jnp.float32) * scale
    p_s = masked_softmax(s_s.reshape(B, Tq, N_KV, GROUP, n_k), mask_s).reshape(s_s.shape)
    o_s = jnp.einsum('bqhgkc,bqhkcd->bqhgd', p_s.astype(v_g.dtype), v_g)
    mask_w = ((kw_pos[None, :] <= q_pos[:, None]) & (kw_pos[None, :] >= q_pos[:, None] - WINDOW)
              & (kw_pos[None, :] >= 0))[None, :, None, None, :]
    s_w = jnp.einsum('bqhgd,bkhd->bqhgk', q, kw).astype(jnp.float32) * scale
    p_w = masked_softmax(s_w, mask_w)
    o_w = jnp.einsum('bqhgk,bkhd->bqhgd', p_w.astype(vw.dtype), vw)
    g = jax.nn.sigmoid(gates.astype(jnp.float32))
    o = (g[..., 0:1] * o_c.astype(jnp.float32) + g[..., 1:2] * o_s.astype(jnp.float32)
         + g[..., 2:3] * o_w.astype(jnp.float32))
    return o.astype(q.dtype)


def nsa_prompt(xn, w_in, q_norm, k_norm, w_cmp, pe_cmp, w_out):
    B, T, _ = xn.shape
    pos = jnp.arange(T, dtype=jnp.int32)
    q, paged, win, gates = nsa_project(xn, pos, w_in, q_norm, k_norm)
    kc = compress(paged[:, :, 0], w_cmp[0], pe_cmp[0])
    vc = compress(paged[:, :, 1], w_cmp[1], pe_cmp[1])
    ksb = sel_blocks(paged[:, :, 2])
    vsb = sel_blocks(paged[:, :, 3])
    win_pad = jnp.pad(win, ((0, 0), (WINDOW, 0), (0, 0), (0, 0), (0, 0)))
    nblk = T // Q_BLOCK
    qb = q.reshape(B, nblk, Q_BLOCK, N_KV, GROUP, HEAD_DIM).swapaxes(0, 1)
    gb = gates.reshape(B, nblk, Q_BLOCK, N_KV, GROUP, 3).swapaxes(0, 1)

    def body(args):
        b, q_blk, g_blk = args
        q0 = b * Q_BLOCK
        q_pos = q0 + jnp.arange(Q_BLOCK, dtype=jnp.int32)
        w_blk = lax.dynamic_slice_in_dim(win_pad, q0, Q_BLOCK + WINDOW, axis=1)
        kw_pos = q0 - WINDOW + jnp.arange(Q_BLOCK + WINDOW, dtype=jnp.int32)
        return nsa_core(q_blk, q_pos, kc, vc, ksb, vsb, w_blk[:, :, 0], w_blk[:, :, 1], kw_pos, g_blk)

    o = lax.map(body, (jnp.arange(nblk, dtype=jnp.int32), qb, gb))
    o = o.swapaxes(0, 1).reshape(B, T, N_HEADS * HEAD_DIM)
    return o @ w_out, paged, win[:, T - min(WINDOW, T):]


def nsa_sample(xn, cache_l, win_buf, page_table, w_in, q_norm, k_norm, w_cmp, pe_cmp, w_out):
    B, S, _ = xn.shape
    past_len = page_table.shape[1] * cache_l.shape[1]
    pos = past_len + jnp.arange(S, dtype=jnp.int32)
    q, paged, win, gates = nsa_project(xn, pos, w_in, q_norm, k_norm)
    past = cache_l[page_table].reshape(B, past_len, 4, N_KV, HEAD_DIM)
    full = jnp.concatenate([past, paged.astype(past.dtype)], axis=1)
    kc = compress(full[:, :, 0], w_cmp[0], pe_cmp[0])
    vc = compress(full[:, :, 1], w_cmp[1], pe_cmp[1])
    ksb = sel_blocks(full[:, :, 2])
    vsb = sel_blocks(full[:, :, 3])
    wb = win_buf.shape[1]
    wcat = jnp.concatenate([win_buf, win.astype(win_buf.dtype)], axis=1)
    kw_pos = past_len - wb + jnp.arange(wb + S, dtype=jnp.int32)
    o = nsa_core(q, pos, kc, vc, ksb, vsb, wcat[:, :, 0], wcat[:, :, 1], kw_pos, gates)
    return o.reshape(B, S, N_HEADS * HEAD_DIM) @ w_out, paged, wcat[:, S:]


def setup_inputs(seed: int = 0) -> dict:
    key = jax.random.key(seed)
    ks = jax.random.split(key, 32)

    def nrm(k, shape, scale):
        return jax.random.normal(k, shape, jnp.float32) * scale

    n_pages = PAST_LEN // PAGE_SIZE
    n_used = DEC_BATCH * n_pages
    n_pool = n_used + max(1, n_used // 4)
    wb = min(WINDOW, PAST_LEN)
    page_table = jax.random.permutation(ks[6], n_pool)[:n_used].reshape(DEC_BATCH, n_pages).astype(jnp.int32)
    dh = N_HEADS * HEAD_DIM
    return {
        "x_prompt": nrm(ks[0], (BATCH, SEQ, D_MODEL), 1.0),
        "x_sample": nrm(ks[1], (DEC_BATCH, DEC_SEQ, D_MODEL), 1.0),
        "state_conv_a": nrm(ks[2], (N_A, DEC_BATCH, CONV_A_WIDTH - 1, D_MODEL), 0.5),
        "state_conv_b": nrm(ks[3], (N_B, DEC_BATCH, CONV_B_WIDTH - 1, D_MODEL), 1.0),
        "cache_kv": nrm(ks[4], (N_C, n_pool, PAGE_SIZE, 4, N_KV, HEAD_DIM), 1.0),
        "state_kv_win": nrm(ks[5], (N_C, DEC_BATCH, wb, 2, N_KV, HEAD_DIM), 1.0),
        "page_table": page_table,
        "norm_mix": 1.0 + nrm(ks[7], (DEPTH, D_MODEL), 0.02),
        "norm_ffn": 1.0 + nrm(ks[8], (DEPTH, D_MODEL), 0.02),
        "a_w_in": nrm(ks[9], (N_A, D_MODEL, 2 * D_MODEL), D_MODEL ** -0.5),
        "a_b_in": nrm(ks[10], (N_A, 2 * D_MODEL), 0.01),
        "a_w_dw": nrm(ks[11], (N_A, CONV_A_WIDTH, D_MODEL), CONV_A_WIDTH ** -0.5),
        "a_b_dw": nrm(ks[12], (N_A, D_MODEL), 0.01),
        "a_ln_g": 1.0 + nrm(ks[13], (N_A, D_MODEL), 0.02),
        "a_ln_b": nrm(ks[14], (N_A, D_MODEL), 0.01),
        "a_w_out": nrm(ks[15], (N_A, D_MODEL, D_MODEL), D_MODEL ** -0.5),
        "a_b_out": nrm(ks[16], (N_A, D_MODEL), 0.01),
        "b_w_in": nrm(ks[17], (N_B, D_MODEL, 3 * D_MODEL), D_MODEL ** -0.5),
        "b_w_dw": nrm(ks[18], (N_B, CONV_B_WIDTH, D_MODEL), CONV_B_WIDTH ** -0.5),
        "b_w_out": nrm(ks[19], (N_B, D_MODEL, D_MODEL), D_MODEL ** -0.5),
        "c_w_in": nrm(ks[20], (N_C, D_MODEL, NSA_IN), D_MODEL ** -0.5),
        "c_q_norm": 1.0 + nrm(ks[21], (N_C, HEAD_DIM), 0.02),
        "c_k_norm": 1.0 + nrm(ks[22], (N_C, 3, HEAD_DIM), 0.02),
        "c_w_cmp": nrm(ks[23], (N_C, 2, CMP_BLOCK * HEAD_DIM, HEAD_DIM), (CMP_BLOCK * HEAD_DIM) ** -0.5),
        "c_pe_cmp": nrm(ks[24], (N_C, 2, CMP_BLOCK, HEAD_DIM), 0.1),
        "c_w_out": nrm(ks[25], (N_C, dh, D_MODEL), dh ** -0.5),
        "ffn_w_gate": nrm(ks[26], (DEPTH, D_MODEL, D_FF), D_MODEL ** -0.5),
        "ffn_w_up": nrm(ks[27], (DEPTH, D_MODEL, D_FF), D_MODEL ** -0.5),
        "ffn_w_down": nrm(ks[28], (DEPTH, D_FF, D_MODEL), D_FF ** -0.5),
    }


def reference(x_prompt, x_sample, state_conv_a, state_conv_b, cache_kv, state_kv_win, page_table,
              norm_mix, norm_ffn, a_w_in, a_b_in, a_w_dw, a_b_dw, a_ln_g, a_ln_b, a_w_out, a_b_out,
              b_w_in, b_w_dw, b_w_out, c_w_in, c_q_norm, c_k_norm, c_w_cmp, c_pe_cmp, c_w_out,
              ffn_w_gate, ffn_w_up, ffn_w_down):
    xp, xs = x_prompt, x_sample
    bp = xp.shape[0]
    conv_a_p, conv_a_s, conv_b_p, conv_b_s = [], [], [], []
    kv_p, kv_s, win_p, win_s = [], [], [], []
    for i in range(DEPTH):
        kind, j = i % N_MIXERS, i // N_MIXERS
        hp = rmsnorm(xp, norm_mix[i])
        hs = rmsnorm(xs, norm_mix[i])
        if kind == 0:
            wa = (a_w_in[j], a_b_in[j], a_w_dw[j], a_b_dw[j], a_ln_g[j], a_ln_b[j], a_w_out[j], a_b_out[j])
            hist0 = jnp.zeros((bp, CONV_A_WIDTH - 1, D_MODEL), hp.dtype)
            yp, st_p = conformer_mix(hp, hist0, *wa)
            ys, st_s = conformer_mix(hs, state_conv_a[j], *wa)
            conv_a_p.append(st_p)
            conv_a_s.append(st_s)
        elif kind == 1:
            wb_ = (b_w_in[j], b_w_dw[j], b_w_out[j])
            hist0 = jnp.zeros((bp, CONV_B_WIDTH - 1, D_MODEL), hp.dtype)
            yp, st_p = gated_conv_mix(hp, hist0, *wb_)
            ys, st_s = gated_conv_mix(hs, state_conv_b[j], *wb_)
            conv_b_p.append(st_p)
            conv_b_s.append(st_s)
        else:
            wc = (c_w_in[j], c_q_norm[j], c_k_norm[j], c_w_cmp[j], c_pe_cmp[j], c_w_out[j])
            yp, rows_p, wst_p = nsa_prompt(hp, *wc)
            ys, rows_s, wst_s = nsa_sample(hs, cache_kv[j], state_kv_win[j], page_table, *wc)
            kv_p.append(rows_p)
            kv_s.append(rows_s)
            win_p.append(wst_p)
            win_s.append(wst_s)
        xp = xp + yp
        xs = xs + ys
        xp = xp + swiglu(rmsnorm(xp, norm_ffn[i]), ffn_w_gate[i], ffn_w_up[i], ffn_w_down[i])
        xs = xs + swiglu(rmsnorm(xs, norm_ffn[i]), ffn_w_gate[i], ffn_w_up[i], ffn_w_down[i])
    y_prompt, y_sample = xp, xs
    new_conv_a_prompt = jnp.stack(conv_a_p)
    new_conv_a_sample = jnp.stack(conv_a_s)
    new_conv_b_prompt = jnp.stack(conv_b_p)
    new_conv_b_sample = jnp.stack(conv_b_s)
    new_kv_prompt = jnp.stack(kv_p)
    new_kv_sample = jnp.stack(kv_s)
    new_win_prompt = jnp.stack(win_p)
    new_win_sample = jnp.stack(win_s)
    return (y_prompt, y_sample, new_conv_a_prompt, new_conv_a_sample, new_conv_b_prompt,
            new_conv_b_sample, new_kv_prompt, new_kv_sample, new_win_prompt, new_win_sample)
```

```python
import functools
import math

import jax
import jax.numpy as jnp
from jax import lax
from jax.experimental import pallas as pl
from jax.experimental.pallas import tpu as pltpu

F32 = jnp.float32
BF16 = jnp.bfloat16

EPS = 1e-6
HEAD_DIM = 64
N_KV = 4
GROUP = 4
N_HEADS = N_KV * GROUP
ROT_DIM = HEAD_DIM // 4
ROPE_THETA = 500000.0
CMP_BLOCK = 32
CMP_STRIDE = 16
SEL_BLOCK = 64
N_SELECT = 16
WINDOW = 512
FORCE_SCORE = 1e9
MASK_BIAS = -1e9
M_INIT = -1e30

LANES = 128
VMEM_LIMIT = 56 * 1024 * 1024


def _cparams(n_grid):
    return pltpu.CompilerParams(dimension_semantics=("arbitrary",) * n_grid,
                                vmem_limit_bytes=VMEM_LIMIT)


def _const_spec(shape):
    nd = len(shape)
    return pl.BlockSpec(shape, lambda *_: (0,) * nd, pipeline_mode=pl.Buffered(1))


def _dot(a, b):
    return jnp.dot(a, b, preferred_element_type=F32)


def _dot_t(a, b):
    return lax.dot_general(a, b, (((1,), (1,)), ((), ())), preferred_element_type=F32)


def _rms(x, g):
    return x * lax.rsqrt(jnp.mean(x * x, axis=-1, keepdims=True) + EPS) * g


def _split3(x):
    hi = x.astype(BF16)
    r = x - hi.astype(F32)
    mid = r.astype(BF16)
    lo = (r - mid.astype(F32)).astype(BF16)
    return hi, mid, lo


def _ffn_body(x_ref, g_ref, wg_ref, wu_ref, wd_ref, o_ref, *, fc):
    x = x_ref[...]
    xn = _rms(x, g_ref[...]).astype(BF16)
    d_ff = wg_ref.shape[1]
    acc = None
    for c in range(d_ff // fc):
        hg = _dot(xn, wg_ref[:, c * fc:(c + 1) * fc])
        hu = _dot(xn, wu_ref[:, c * fc:(c + 1) * fc])
        a = (hg * jax.nn.sigmoid(hg) * hu).astype(BF16)
        d = _dot(a, wd_ref[c * fc:(c + 1) * fc, :])
        acc = d if acc is None else acc + d
    o_ref[...] = x + acc


def _ffn(x, g, wg, wu, wd, tm):
    m, d = x.shape
    d_ff = wg.shape[1]
    return pl.pallas_call(
        functools.partial(_ffn_body, fc=256),
        grid=(m // tm,),
        in_specs=[pl.BlockSpec((tm, d), lambda i: (i, 0)),
                  _const_spec((1, d)), _const_spec((d, d_ff)), _const_spec((d, d_ff)),
                  _const_spec((d_ff, d))],
        out_specs=pl.BlockSpec((tm, d), lambda i: (i, 0)),
        out_shape=jax.ShapeDtypeStruct((m, d), F32),
        compiler_params=_cparams(1), name="ffn",
    )(x, g, wg, wu, wd)


def _linres_body(x_ref, o_ref, w_ref, y_ref):
    y_ref[...] = x_ref[...] + _dot(o_ref[...], w_ref[...])


def _linres(x, o, w, tm):
    m, d = x.shape
    k = o.shape[1]
    return pl.pallas_call(
        _linres_body, grid=(m // tm,),
        in_specs=[pl.BlockSpec((tm, d), lambda i: (i, 0)),
                  pl.BlockSpec((tm, k), lambda i: (i, 0)), _const_spec((k, d))],
        out_specs=pl.BlockSpec((tm, d), lambda i: (i, 0)),
        out_shape=jax.ShapeDtypeStruct((m, d), F32),
        compiler_params=_cparams(1), name="linres",
    )(x, o, w)


A_HALO = 32
A_CHUNK = 32


def _ln_swish(c, g, b):
    mu = jnp.mean(c, axis=-1, keepdims=True)
    var = jnp.mean(jnp.square(c - mu), axis=-1, keepdims=True)
    y = (c - mu) * lax.rsqrt(var + EPS) * g + b
    return y * jax.nn.sigmoid(y)


def _mixa_prompt_body(x_ref, g_ref, win_ref, bin_ref, wdw_ref, bdw_ref, lng_ref, lnb_ref,
                      wout_ref, bout_ref, y_ref, tail_ref, ubuf, cbuf, *, width):
    i = pl.program_id(1)
    tm, d = x_ref.shape[1], x_ref.shape[2]

    @pl.when(i == 0)
    def _():
        ubuf[0:A_HALO, :] = jnp.zeros((A_HALO, d), F32)

    x = x_ref[0]
    xn = _rms(x, g_ref[...]).astype(BF16)
    h = _dot(xn, win_ref[...]) + bin_ref[...]
    u = h[:, :d] * jax.nn.sigmoid(h[:, d:])
    ubuf[A_HALO:, :] = u
    off = A_HALO - (width - 1)
    for r0 in range(0, tm, A_CHUNK):
        acc = None
        for k in range(width):
            t = wdw_ref[k:k + 1, :] * ubuf[r0 + off + k:r0 + off + k + A_CHUNK, :]
            acc = t if acc is None else acc + t
        c = _ln_swish(acc + bdw_ref[...], lng_ref[...], lnb_ref[...])
        cbuf[r0:r0 + A_CHUNK, :] = c.astype(BF16)
    y_ref[0] = x + _dot(cbuf[...], wout_ref[...]) + bout_ref[...]
    last = ubuf[tm:tm + A_HALO, :]
    ubuf[0:A_HALO, :] = last

    @pl.when(i == pl.num_programs(1) - 1)
    def _():
        tail_ref[0] = last


def _mixa_prompt(x, g, w_in, b_in, w_dw, b_dw, ln_g, ln_b, w_out, b_out, tm):
    b, t, d = x.shape
    width = w_dw.shape[0]
    y, tail = pl.pallas_call(
        functools.partial(_mixa_prompt_body, width=width),
        grid=(b, t // tm),
        in_specs=[pl.BlockSpec((1, tm, d), lambda bi, i: (bi, i, 0)),
                  _const_spec((1, d)), _const_spec((d, 2 * d)), _const_spec((1, 2 * d)),
                  _const_spec((width, d)), _const_spec((1, d)), _const_spec((1, d)),
                  _const_spec((1, d)), _const_spec((d, d)), _const_spec((1, d))],
        out_specs=[pl.BlockSpec((1, tm, d), lambda bi, i: (bi, i, 0)),
                   pl.BlockSpec((1, A_HALO, d), lambda bi, i: (bi, 0, 0))],
        out_shape=[jax.ShapeDtypeStruct((b, t, d), F32),
                   jax.ShapeDtypeStruct((b, A_HALO, d), F32)],
        scratch_shapes=[pltpu.VMEM((tm + A_HALO, d), F32), pltpu.VMEM((tm, d), BF16)],
        compiler_params=_cparams(2), name="mixa_prompt",
    )(x, g, w_in, b_in, w_dw, b_dw, ln_g, ln_b, w_out, b_out)
    return y, tail[:, A_HALO - (width - 1):]


def _mixa_sample_body(x_ref, hist_ref, g_ref, win_ref, bin_ref, wdw_ref, bdw_ref, lng_ref,
                      lnb_ref, wout_ref, bout_ref, y_ref, st_ref, *, width):
    d = x_ref.shape[1]
    x = x_ref[...]
    xn = _rms(x, g_ref[...]).astype(BF16)
    h = _dot(xn, win_ref[...]) + bin_ref[...]
    u = h[:, :d] * jax.nn.sigmoid(h[:, d:])
    acc = None
    for k in range(width - 1):
        row = hist_ref[:, k, :]
        t = wdw_ref[k:k + 1, :] * row
        acc = t if acc is None else acc + t
        if k >= 1:
            st_ref[:, k - 1, :] = row
    acc = acc + wdw_ref[width - 1:width, :] * u
    st_ref[:, width - 2, :] = u
    c = _ln_swish(acc + bdw_ref[...], lng_ref[...], lnb_ref[...])
    y_ref[...] = x + _dot(c.astype(BF16), wout_ref[...]) + bout_ref[...]


def _mixa_sample(x, hist, g, w_in, b_in, w_dw, b_dw, ln_g, ln_b, w_out, b_out):
    n, d = x.shape
    width = w_dw.shape[0]
    return pl.pallas_call(
        functools.partial(_mixa_sample_body, width=width),
        out_shape=[jax.ShapeDtypeStruct((n, d), F32),
                   jax.ShapeDtypeStruct((n, width - 1, d), F32)],
        compiler_params=pltpu.CompilerParams(vmem_limit_bytes=VMEM_LIMIT), name="mixa_sample",
    )(x, hist, g, w_in, b_in, w_dw, b_dw, ln_g, ln_b, w_out, b_out)


B_HALO = 8


def _mixb_prompt_body(x_ref, g_ref, win_ref, wdw_ref, wout_ref, y_ref, tail_ref, cbuf, *, width):
    i = pl.program_id(1)
    tm, d = x_ref.shape[1], x_ref.shape[2]

    @pl.when(i == 0)
    def _():
        cbuf[0:B_HALO, :] = jnp.zeros((B_HALO, d), F32)

    x = x_ref[0]
    xn = _rms(x, g_ref[...]).astype(BF16)
    h = _dot(xn, win_ref[...])
    cbuf[B_HALO:, :] = h[:, d:2 * d] * h[:, 2 * d:]
    off = B_HALO - (width - 1)
    conv = None
    for k in range(width):
        t = wdw_ref[k:k + 1, :] * cbuf[off + k:off + k + tm, :]
        conv = t if conv is None else conv + t
    z = (h[:, :d] * conv).astype(BF16)
    y_ref[0] = x + _dot(z, wout_ref[...])
    last = cbuf[tm:tm + B_HALO, :]
    cbuf[0:B_HALO, :] = last

    @pl.when(i == pl.num_programs(1) - 1)
    def _():
        tail_ref[0] = last


def _mixb_prompt(x, g, w_in, w_dw, w_out, tm):
    b, t, d = x.shape
    width = w_dw.shape[0]
    y, tail = pl.pallas_call(
        functools.partial(_mixb_prompt_body, width=width),
        grid=(b, t // tm),
        in_specs=[pl.BlockSpec((1, tm, d), lambda bi, i: (bi, i, 0)),
                  _const_spec((1, d)), _const_spec((d, 3 * d)), _const_spec((width, d)),
                  _const_spec((d, d))],
        out_specs=[pl.BlockSpec((1, tm, d), lambda bi, i: (bi, i, 0)),
                   pl.BlockSpec((1, B_HALO, d), lambda bi, i: (bi, 0, 0))],
        out_shape=[jax.ShapeDtypeStruct((b, t, d), F32),
                   jax.ShapeDtypeStruct((b, B_HALO, d), F32)],
        scratch_shapes=[pltpu.VMEM((tm + B_HALO, d), F32)],
        compiler_params=_cparams(2), name="mixb_prompt",
    )(x, g, w_in, w_dw, w_out)
    return y, tail[:, B_HALO - (width - 1):]


def _mixb_sample_body(x_ref, hist_ref, g_ref, win_ref, wdw_ref, wout_ref, y_ref, st_ref, *, width):
    d = x_ref.shape[1]
    x = x_ref[...]
    xn = _rms(x, g_ref[...]).astype(BF16)
    h = _dot(xn, win_ref[...])
    cv = h[:, d:2 * d] * h[:, 2 * d:]
    conv = None
    for k in range(width - 1):
        row = hist_ref[:, k, :]
        t = wdw_ref[k:k + 1, :] * row
        conv = t if conv is None else conv + t
        if k >= 1:
            st_ref[:, k - 1, :] = row
    conv = conv + wdw_ref[width - 1:width, :] * cv
    st_ref[:, width - 2, :] = cv
    z = (h[:, :d] * conv).astype(BF16)
    y_ref[...] = x + _dot(z, wout_ref[...])


def _mixb_sample(x, hist, g, w_in, w_dw, w_out):
    n, d = x.shape
    width = w_dw.shape[0]
    return pl.pallas_call(
        functools.partial(_mixb_sample_body, width=width),
        out_shape=[jax.ShapeDtypeStruct((n, d), F32),
                   jax.ShapeDtypeStruct((n, width - 1, d), F32)],
        compiler_params=pltpu.CompilerParams(vmem_limit_bytes=VMEM_LIMIT), name="mixb_sample",
    )(x, hist, g, w_in, w_dw, w_out)


def _rope_tables(pos):
    half = ROT_DIM // 2
    inv = jnp.exp(jnp.arange(half, dtype=F32) * (-2.0 * math.log(ROPE_THETA) / ROT_DIM))
    ang = pos.astype(F32)[:, None] * inv[None, :]
    cos, sin = jnp.cos(ang), jnp.sin(ang)
    r = jnp.arange(LANES) % HEAD_DIM
    c = jnp.where(r[None, :] < ROT_DIM, cos[:, r % half], 1.0)
    s1 = jnp.where(r[None, :] < half, -sin[:, r % half], 0.0)
    s2 = jnp.where((r[None, :] >= half) & (r[None, :] < ROT_DIM), sin[:, r % half], 0.0)
    return c, s1, s2


def _head_norm_rope(blk, gvec, c, s1, s2):
    lane = lax.broadcasted_iota(jnp.int32, (1, LANES), 1)
    first = lane < HEAD_DIM
    x2 = blk * blk
    ms_a = jnp.sum(jnp.where(first, x2, 0.0), axis=-1, keepdims=True)
    ms_b = jnp.sum(jnp.where(first, 0.0, x2), axis=-1, keepdims=True)
    ms = jnp.where(first, ms_a, ms_b) * (1.0 / HEAD_DIM)
    y = blk * lax.rsqrt(ms + EPS) * gvec
    return y * c + pltpu.roll(y, LANES - ROT_DIM // 2, 1) * s1 + pltpu.roll(y, ROT_DIM // 2, 1) * s2


def _nsa_proj_body(x_ref, g_ref, wqkv_ref, wg_ref, qn_ref, kn_ref, c_ref, s1_ref, s2_ref,
                   q_ref, paged_ref, win_ref, cmp_ref, ksel_ref, vsel_ref, kwin_ref, vwin_ref,
                   gate_ref):
    d = x_ref.shape[2]
    x = x_ref[0]
    xn = _rms(x, g_ref[...]).astype(BF16)
    h = _dot(xn, wqkv_ref[...])
    gh = _dot(xn, wg_ref[...])
    c, s1, s2 = c_ref[...], s1_ref[...], s2_ref[...]
    nq = N_HEADS * HEAD_DIM
    kvw = N_KV * HEAD_DIM
    scale = HEAD_DIM ** -0.5
    for cb in range(nq // LANES):
        blk = _head_norm_rope(h[:, cb * LANES:(cb + 1) * LANES], qn_ref[...], c, s1, s2)
        blk = (blk * scale).astype(BF16)
        q_ref[0, 2 * cb] = blk[:, :HEAD_DIM]
        q_ref[0, 2 * cb + 1] = blk[:, HEAD_DIM:]
    for br in range(3):
        kbase = nq + br * 2 * kvw
        for cb in range(kvw // LANES):
            kblk = _head_norm_rope(h[:, kbase + cb * LANES:kbase + (cb + 1) * LANES],
                                   kn_ref[br:br + 1, :], c, s1, s2)
            vblk = h[:, kbase + kvw + cb * LANES:kbase + kvw + (cb + 1) * LANES]
            if br < 2:
                paged_ref[0, :, br * 2 * kvw + cb * LANES:br * 2 * kvw + (cb + 1) * LANES] = kblk
                paged_ref[0, :, br * 2 * kvw + kvw + cb * LANES:
                          br * 2 * kvw + kvw + (cb + 1) * LANES] = vblk
            else:
                win_ref[0, :, cb * LANES:(cb + 1) * LANES] = kblk
                win_ref[0, :, kvw + cb * LANES:kvw + (cb + 1) * LANES] = vblk
            if br == 0:
                cmp_ref[0, cb] = kblk
                cmp_ref[0, kvw // LANES + cb] = vblk
            else:
                kdst, vdst = (ksel_ref, vsel_ref) if br == 1 else (kwin_ref, vwin_ref)
                kb, vb = kblk.astype(BF16), vblk.astype(BF16)
                kdst[0, 2 * cb] = kb[:, :HEAD_DIM]
                kdst[0, 2 * cb + 1] = kb[:, HEAD_DIM:]
                vdst[0, 2 * cb] = vb[:, :HEAD_DIM]
                vdst[0, 2 * cb + 1] = vb[:, HEAD_DIM:]
    for hh in range(N_KV):
        gate_ref[0, hh] = gh[:, hh * LANES:(hh + 1) * LANES]


def _nsa_proj(x, g, wqkv, wgate, qn, kn, rope, tm):
    b, t, d = x.shape
    c, s1, s2 = rope
    nqkv = wqkv.shape[1]
    hm = lambda n, dt: jax.ShapeDtypeStruct((b, n, t, HEAD_DIM), dt)
    hm_spec = lambda n: pl.BlockSpec((1, n, tm, HEAD_DIM), lambda bi, i: (bi, 0, i, 0))
    row_spec = lambda w: pl.BlockSpec((1, tm, w), lambda bi, i: (bi, i, 0))
    tab_spec = pl.BlockSpec((tm, LANES), lambda bi, i: (i, 0))
    lane_spec = pl.BlockSpec((1, 4, tm, LANES), lambda bi, i: (bi, 0, i, 0))
    return pl.pallas_call(
        _nsa_proj_body, grid=(b, t // tm),
        in_specs=[row_spec(d), _const_spec((1, d)), _const_spec((d, nqkv)),
                  _const_spec((d, N_KV * LANES)), _const_spec((1, LANES)), _const_spec((3, LANES)),
                  tab_spec, tab_spec, tab_spec],
        out_specs=[hm_spec(N_HEADS), row_spec(4 * N_KV * HEAD_DIM), row_spec(2 * N_KV * HEAD_DIM),
                   lane_spec, hm_spec(N_KV), hm_spec(N_KV), hm_spec(N_KV), hm_spec(N_KV), lane_spec],
        out_shape=[hm(N_HEADS, BF16),
                   jax.ShapeDtypeStruct((b, t, 4 * N_KV * HEAD_DIM), F32),
                   jax.ShapeDtypeStruct((b, t, 2 * N_KV * HEAD_DIM), F32),
                   jax.ShapeDtypeStruct((b, 4, t, LANES), F32),
                   hm(N_KV, BF16), hm(N_KV, BF16), hm(N_KV, BF16), hm(N_KV, BF16),
                   jax.ShapeDtypeStruct((b, N_KV, t, LANES), F32)],
        compiler_params=_cparams(2), name="nsa_proj",
    )(x, g, wqkv, wgate, qn, kn, c, s1, s2)


def _cmp_const(w_ref, pe_ref, kv):
    cst = jnp.sum(w_ref[kv] * pe_ref[kv], axis=0, keepdims=True)
    return jnp.concatenate([cst] * N_KV, axis=1)


SHIFT_PAD = 8


def _compress_rows(load, bd_ref, kv, n_blk, shift_buf):
    first = second = None
    for l in range(CMP_STRIDE):
        xl = jnp.concatenate([load(0, l), load(1, l)], axis=1).astype(BF16)
        ta = _dot(xl, bd_ref[kv, l])
        tb = _dot(xl, bd_ref[kv, CMP_STRIDE + l])
        first = ta if first is None else first + ta
        second = tb if second is None else second + tb
    shift_buf[0:n_blk, :] = second
    shift_buf[n_blk:n_blk + SHIFT_PAD, :] = jnp.zeros((SHIFT_PAD, shift_buf.shape[1]), F32)
    return first + shift_buf[1:n_blk + 1, :]


def _compress_body(x_ref, bd_ref, w_ref, pe_ref, o_ref, shift_buf):
    kv = pl.program_id(1)
    n_blk = x_ref.shape[2] // CMP_STRIDE
    out = _compress_rows(lambda c, l: x_ref[0, c, pl.ds(l, n_blk, stride=CMP_STRIDE), :],
                         bd_ref, kv, n_blk, shift_buf) + _cmp_const(w_ref, pe_ref, kv)
    for hh in range(N_KV):
        o_ref[0, 0, hh] = out[:, hh * HEAD_DIM:(hh + 1) * HEAD_DIM].astype(BF16)


def _compress(cmp_rows, bd, w_cmp, pe_col):
    b, _, t, _ = cmp_rows.shape
    n_blk = t // CMP_STRIDE
    return pl.pallas_call(
        _compress_body, grid=(b, 2),
        in_specs=[pl.BlockSpec((1, 2, t, LANES), lambda bi, kv: (bi, kv, 0, 0)),
                  _const_spec(bd.shape), _const_spec(w_cmp.shape), _const_spec(pe_col.shape)],
        out_specs=pl.BlockSpec((1, 1, N_KV, n_blk, HEAD_DIM), lambda bi, kv: (bi, kv, 0, 0, 0)),
        out_shape=jax.ShapeDtypeStruct((b, 2, N_KV, n_blk, HEAD_DIM), BF16),
        scratch_shapes=[pltpu.VMEM((n_blk + SHIFT_PAD, N_KV * HEAD_DIM), F32)],
        compiler_params=_cparams(2), name="nsa_compress",
    )(cmp_rows, bd, w_cmp, pe_col)


def _softmax_rows(s, mask):
    s = jnp.where(mask, s, -jnp.inf)
    m = jnp.max(s, axis=-1, keepdims=True)
    m = jnp.where(m == -jnp.inf, 0.0, m)
    e = jnp.exp(s - m)
    return e / jnp.maximum(jnp.sum(e, axis=-1, keepdims=True), 1e-30)


def _importance(p_sum, ov_ref):
    hi, mid, lo = _split3(p_sum)
    return _dot(hi, ov_ref[...]) + _dot(mid, ov_ref[...]) + _dot(lo, ov_ref[...])


def _select_blocks(imp, q_pos, n_sel):
    rows, w = imp.shape
    j = lax.broadcasted_iota(jnp.int32, (rows, w), 1)
    cur = q_pos // SEL_BLOCK
    forced = (j == 0) | (j == cur) | (j == cur - 1)
    v = jnp.where(forced, FORCE_SCORE, imp)
    v = jnp.where((j * SEL_BLOCK > q_pos) | (j >= n_sel), -jnp.inf, v)
    sel = jnp.zeros((rows, w), jnp.bool_)
    for _ in range(min(N_SELECT, n_sel)):
        mx = jnp.max(v, axis=-1, keepdims=True)
        idx = jnp.min(jnp.where(v == mx, j, w), axis=-1, keepdims=True)
        hit = (j == idx) & (mx > -jnp.inf)
        sel = sel | hit
        v = jnp.where(hit, -jnp.inf, v)
    return sel


def _nsa_attn_body(q_ref, kc_ref, vc_ref, ks_ref, vs_ref, kw_ref, vw_ref, g_ref, ov_ref, e_ref,
                   o_ref, *, tq, tk):
    t0 = pl.program_id(2) * tq
    rows = GROUP * tq
    n_cmp = kc_ref.shape[3]
    t_len = ks_ref.shape[2]
    q = q_ref[0].reshape(rows, HEAD_DIM)
    row_t = t0 + lax.rem(lax.broadcasted_iota(jnp.int32, (rows, 1), 0), tq)

    ci = lax.broadcasted_iota(jnp.int32, (1, n_cmp), 1)
    p_c = _softmax_rows(_dot_t(q, kc_ref[0, 0, 0]), ci * CMP_STRIDE + (CMP_BLOCK - 1) <= row_t)
    o_c = _dot(p_c.astype(BF16), vc_ref[0, 0, 0])
    p_sum = p_c[0:tq]
    for g in range(1, GROUP):
        p_sum = p_sum + p_c[g * tq:(g + 1) * tq]

    q_pos = t0 + lax.broadcasted_iota(jnp.int32, (tq, 1), 0)
    sel = _select_blocks(_importance(p_sum, ov_ref), q_pos, t_len // SEL_BLOCK)
    bias = jnp.where(sel, 0.0, MASK_BIAS).astype(BF16)
    bias = jnp.concatenate([bias] * GROUP, axis=0)

    def sel_tile(kt, carry, causal):
        m, l, acc = carry
        k0 = pl.multiple_of(kt * tk, tk)
        s = _dot_t(q, ks_ref[0, 0, pl.ds(k0, tk), :]) + _dot_t(bias, e_ref[pl.ds(k0, tk), :])
        if causal:
            kpos = k0 + lax.broadcasted_iota(jnp.int32, (1, tk), 1)
            s = jnp.where(kpos <= row_t, s, MASK_BIAS)
        m_new = jnp.maximum(m, jnp.max(s, axis=-1, keepdims=True))
        alpha = jnp.exp(m - m_new)
        p = jnp.exp(s - m_new)
        l = alpha * l + jnp.sum(p, axis=-1, keepdims=True)
        acc = alpha * acc + _dot(p.astype(BF16), vs_ref[0, 0, pl.ds(k0, tk), :])
        return m_new, l, acc

    kt_diag = t0 // tk
    carry = (jnp.full((rows, 1), M_INIT, F32), jnp.zeros((rows, 1), F32),
             jnp.zeros((rows, HEAD_DIM), F32))
    carry = lax.fori_loop(0, kt_diag, functools.partial(sel_tile, causal=False), carry)
    _, l_s, acc_s = sel_tile(kt_diag, carry, True)
    o_s = acc_s / l_s

    nw = WINDOW + tq
    w0 = pl.multiple_of(jnp.maximum(t0 - WINDOW, 0), tq)
    kpos = w0 + lax.broadcasted_iota(jnp.int32, (1, nw), 1)
    p_w = _softmax_rows(_dot_t(q, kw_ref[0, 0, pl.ds(w0, nw), :]),
                        (kpos <= row_t) & (kpos >= row_t - WINDOW))
    o_w = _dot(p_w.astype(BF16), vw_ref[0, 0, pl.ds(w0, nw), :])

    gate = jax.nn.sigmoid(g_ref[0, 0])
    outs = []
    for g in range(GROUP):
        sl = slice(g * tq, (g + 1) * tq)
        outs.append(gate[:, 3 * g:3 * g + 1] * o_c[sl] + gate[:, 3 * g + 1:3 * g + 2] * o_s[sl]
                    + gate[:, 3 * g + 2:3 * g + 3] * o_w[sl])
    o_ref[0] = jnp.concatenate(outs, axis=1).astype(BF16)


def _nsa_attn(q_hm, kvc, ks, vs, kw, vw, gates, overlap, onehot, tq, tk):
    b, _, t, _ = q_hm.shape
    n_cmp = kvc.shape[3]
    kv_spec = lambda n: pl.BlockSpec((1, 1, n, HEAD_DIM), lambda bi, h, i: (bi, h, 0, 0))
    cmp_spec = lambda kv: pl.BlockSpec((1, 1, 1, n_cmp, HEAD_DIM), lambda bi, h, i: (bi, kv, h, 0, 0))
    return pl.pallas_call(
        functools.partial(_nsa_attn_body, tq=tq, tk=tk),
        grid=(b, N_KV, t // tq),
        in_specs=[pl.BlockSpec((1, GROUP, tq, HEAD_DIM), lambda bi, h, i: (bi, h, i, 0)),
                  cmp_spec(0), cmp_spec(1), kv_spec(t), kv_spec(t), kv_spec(t), kv_spec(t),
                  pl.BlockSpec((1, 1, tq, LANES), lambda bi, h, i: (bi, h, i, 0)),
                  _const_spec(overlap.shape), _const_spec(onehot.shape)],
        out_specs=pl.BlockSpec((1, tq, GROUP * HEAD_DIM), lambda bi, h, i: (bi, i, h)),
        out_shape=jax.ShapeDtypeStruct((b, t, N_HEADS * HEAD_DIM), BF16),
        compiler_params=_cparams(3), name="nsa_attn",
    )(q_hm, kvc, kvc, ks, vs, kw, vw, gates, overlap, onehot)


S_ROWS = 32
S_HPAD = 8


def _nsa_sample_body(pt_ref, *refs, n_pg, past_len):
    pages = refs[:n_pg]
    (win_ref, q_ref, new_ref, wnew_ref, g_ref, bd_ref, w_ref, pe_ref, ov_ref, e_ref,
     o_ref, wout_ref, kcbuf, vcbuf, ksbuf, vsbuf, shift_buf) = refs[n_pg:]
    step = pl.program_id(1)
    kvw = N_KV * HEAD_DIM
    psz = pages[0].shape[1]
    for i in range(n_pg):
        r0 = pl.multiple_of((step * n_pg + i) * psz, psz)
        for c in range(2):
            kcbuf[c, pl.ds(r0, psz), :] = pages[i][0, :, c * LANES:(c + 1) * LANES]
            vcbuf[c, pl.ds(r0, psz), :] = pages[i][0, :, kvw + c * LANES:kvw + (c + 1) * LANES]
        ksbuf[pl.ds(r0, psz), :] = pages[i][0, :, 2 * kvw:3 * kvw].astype(BF16)
        vsbuf[pl.ds(r0, psz), :] = pages[i][0, :, 3 * kvw:4 * kvw].astype(BF16)

    @pl.when(step == pl.num_programs(1) - 1)
    def _():
        n_blk = past_len // CMP_STRIDE
        new = new_ref[0]
        cmp_of = lambda buf, kv: (_compress_rows(
            lambda c, l: buf[c, pl.ds(l, n_blk, stride=CMP_STRIDE), :], bd_ref, kv, n_blk, shift_buf)
            + _cmp_const(w_ref, pe_ref, kv)).astype(BF16)
        kc = cmp_of(kcbuf, 0)
        vc = cmp_of(vcbuf, 1)

        row = lax.broadcasted_iota(jnp.int32, (S_ROWS, kvw), 0)
        lane = lax.broadcasted_iota(jnp.int32, (S_ROWS, kvw), 1)
        own = lane // HEAD_DIM == lax.rem(row, S_HPAD)
        q = jnp.where(own, jnp.concatenate([q_ref[0]] * N_KV, axis=1), 0.0).astype(BF16)
        qf = q.astype(F32)

        def own_head(full):
            z = jnp.where(own, full, 0.0)
            out = z[:, 0:HEAD_DIM]
            for hh in range(1, N_KV):
                out = out + z[:, hh * HEAD_DIM:(hh + 1) * HEAD_DIM]
            return out

        def score_new(k_row):
            return jnp.sum(qf * k_row.astype(BF16).astype(F32), axis=-1, keepdims=True)

        def attend(s, mask, v, s_new, v_row):
            s = jnp.where(mask, s, -jnp.inf)
            m = jnp.maximum(jnp.max(s, axis=-1, keepdims=True), s_new)
            e = jnp.exp(s - m)
            e_new = jnp.exp(s_new - m)
            den = jnp.sum(e, axis=-1, keepdims=True) + e_new
            p = e / den
            p_new = (e_new / den).astype(BF16).astype(F32)
            return own_head(_dot(p.astype(BF16), v) + p_new * v_row.astype(BF16).astype(F32))

        q_pos = past_len
        ci = lax.broadcasted_iota(jnp.int32, (1, n_blk), 1)
        p_c = _softmax_rows(_dot_t(q, kc), ci * CMP_STRIDE + (CMP_BLOCK - 1) <= q_pos)
        o_c = own_head(_dot(p_c.astype(BF16), vc))
        p_sum = p_c[0:S_HPAD]
        for g in range(1, GROUP):
            p_sum = p_sum + p_c[g * S_HPAD:(g + 1) * S_HPAD]

        n_sel = past_len // SEL_BLOCK + 1
        sel = _select_blocks(_importance(p_sum, ov_ref), jnp.full((S_HPAD, 1), q_pos, jnp.int32), n_sel)
        bias = jnp.where(sel, 0.0, MASK_BIAS).astype(BF16)
        bias = jnp.concatenate([bias] * GROUP, axis=0)

        s_s = _dot_t(q, ksbuf[...]) + _dot_t(bias[:, 0:e_ref.shape[1]], e_ref[...])
        o_s = attend(s_s, s_s > 0.5 * MASK_BIAS, vsbuf[...],
                     score_new(new[:, 2 * kvw:3 * kvw]), new[:, 3 * kvw:4 * kvw])

        wnew = wnew_ref[0]
        kw = win_ref[0, :, 0:kvw].astype(BF16)
        vw = win_ref[0, :, kvw:2 * kvw].astype(BF16)
        s_w = _dot_t(q, kw)
        o_w = attend(s_w, s_w == s_w, vw, score_new(wnew[:, 0:kvw]), wnew[:, kvw:2 * kvw])

        gate = jax.nn.sigmoid(g_ref[0])
        o_ref[0] = (gate[:, 0:1] * o_c + gate[:, 1:2] * o_s + gate[:, 2:3] * o_w).astype(BF16)

        wb = win_ref.shape[1]
        wout_ref[0, 0:wb - 1, :] = win_ref[0, 1:wb, :]
        wout_ref[0, wb - 1:wb, :] = wnew


def _nsa_sample(page_table, cache, win_buf, q_rows, new_rows, wnew_rows, gate_rows,
                bd, w_cmp, pe_col, overlap, onehot, n_pg):
    nb, n_pages = page_table.shape
    _, psz, roww = cache.shape
    past_len = n_pages * psz
    kvw = N_KV * HEAD_DIM
    wb = win_buf.shape[1]

    def page_spec(i):
        return pl.BlockSpec((1, psz, roww), lambda b, s, pt: (pt[b, s * n_pg + i], 0, 0))

    per_b = lambda shape: pl.BlockSpec((1,) + shape, lambda b, s, pt: (b, 0, 0))
    cst = lambda a: pl.BlockSpec(a.shape, lambda b, s, pt: (0,) * a.ndim, pipeline_mode=pl.Buffered(1))
    grid_spec = pltpu.PrefetchScalarGridSpec(
        num_scalar_prefetch=1, grid=(nb, n_pages // n_pg),
        in_specs=[page_spec(i) for i in range(n_pg)] + [
            per_b((wb, 2 * kvw)), per_b((S_ROWS, HEAD_DIM)), per_b((1, roww)), per_b((1, 2 * kvw)),
            per_b((S_ROWS, LANES)), cst(bd), cst(w_cmp), cst(pe_col), cst(overlap), cst(onehot)],
        out_specs=[per_b((S_ROWS, HEAD_DIM)), per_b((wb, 2 * kvw))],
        scratch_shapes=[pltpu.VMEM((2, past_len, LANES), F32), pltpu.VMEM((2, past_len, LANES), F32),
                        pltpu.VMEM((past_len, kvw), BF16), pltpu.VMEM((past_len, kvw), BF16),
                        pltpu.VMEM((past_len // CMP_STRIDE + SHIFT_PAD, kvw), F32)])
    return pl.pallas_call(
        functools.partial(_nsa_sample_body, n_pg=n_pg, past_len=past_len),
        grid_spec=grid_spec,
        out_shape=[jax.ShapeDtypeStruct((nb, S_ROWS, HEAD_DIM), BF16),
                   jax.ShapeDtypeStruct((nb, wb, 2 * kvw), F32)],
        compiler_params=_cparams(2), name="nsa_sample",
    )(page_table, *([cache] * n_pg), win_buf, q_rows, new_rows, wnew_rows, gate_rows,
      bd, w_cmp, pe_col, overlap, onehot)


def _overlap_matrix(n_cmp, width):
    ci = jnp.arange(n_cmp)[:, None] * CMP_STRIDE
    start = jnp.arange(width)[None, :] * SEL_BLOCK
    return ((ci < start + SEL_BLOCK) & (ci + CMP_BLOCK > start)).astype(BF16)


def _block_onehot(t, width):
    return (jnp.arange(t)[:, None] // SEL_BLOCK == jnp.arange(width)[None, :]).astype(BF16)


def _nsa_weights(w_in, q_norm, k_norm, w_cmp, pe_cmp):
    nq = N_HEADS * HEAD_DIM
    nkv = 6 * N_KV * HEAD_DIM
    d = w_in.shape[0]
    wqkv = w_in[:, :nq + nkv].astype(BF16)
    wg = w_in[:, nq + nkv:].reshape(d, N_KV, GROUP * 3)
    wg = jnp.pad(wg, ((0, 0), (0, 0), (0, LANES - GROUP * 3))).reshape(d, N_KV * LANES).astype(BF16)
    qn = jnp.tile(q_norm, LANES // HEAD_DIM)[None, :]
    kn = jnp.tile(k_norm, (1, LANES // HEAD_DIM))
    w4 = w_cmp.reshape(2, CMP_BLOCK, HEAD_DIM, HEAD_DIM)
    eye = jnp.eye(N_KV, dtype=w_cmp.dtype)
    bd = (eye[None, None, :, None, :, None] * w4[:, :, None, :, None, :]).reshape(
        2, CMP_BLOCK, N_KV * HEAD_DIM, N_KV * HEAD_DIM).astype(BF16)
    pe_col = pe_cmp.reshape(2, CMP_BLOCK * HEAD_DIM, 1)
    return wqkv, wg, qn, kn, bd, pe_col


def _nsa_layer(xp, xs, g, cache_l, win_buf, page_table, w_in, q_norm, k_norm, w_cmp, pe_cmp, w_out):
    b, t, d = xp.shape
    nb = xs.shape[0]
    wqkv, wg, qn, kn, bd, pe_col = _nsa_weights(w_in, q_norm, k_norm, w_cmp, pe_cmp)
    w_out_b = w_out.astype(BF16)
    g2 = g[None, :]

    rope_p = _rope_tables(jnp.arange(t, dtype=jnp.int32))
    q_hm, paged, win, cmp_rows, ks, vs, kw, vw, gates = _nsa_proj(xp, g2, wqkv, wg, qn, kn, rope_p, 256)
    kvc = _compress(cmp_rows, bd, w_cmp, pe_col)
    n_cmp = t // CMP_STRIDE
    o = _nsa_attn(q_hm, kvc, ks, vs, kw, vw, gates, _overlap_matrix(n_cmp, LANES),
                  _block_onehot(t, LANES), 128, 512)
    yp = _linres(xp.reshape(b * t, d), o.reshape(b * t, d), w_out_b, 512).reshape(b, t, d)
    win_p = win[:, t - min(WINDOW, t):]

    n_pages = page_table.shape[1]
    psz = cache_l.shape[1]
    past_len = n_pages * psz
    rope_s = _rope_tables(jnp.full((nb,), past_len, jnp.int32))
    q_s, paged_s, win_s, _, _, _, _, _, gates_s = _nsa_proj(xs[None], g2, wqkv, wg, qn, kn, rope_s, nb)
    qr = q_s[0].reshape(N_KV, GROUP, nb, HEAD_DIM).transpose(2, 1, 0, 3)
    qr = jnp.pad(qr, ((0, 0), (0, 0), (0, S_HPAD - N_KV), (0, 0))).reshape(nb, S_ROWS, HEAD_DIM)
    gr = gates_s[0][:, :, :GROUP * 3].reshape(N_KV, nb, GROUP, 3).transpose(1, 2, 0, 3)
    gr = jnp.pad(gr, ((0, 0), (0, 0), (0, S_HPAD - N_KV), (0, LANES - 3))).reshape(nb, S_ROWS, LANES)
    n_selw = 2 * LANES
    o_s, win_new = _nsa_sample(
        page_table, cache_l.reshape(cache_l.shape[0], psz, -1), win_buf.reshape(nb, win_buf.shape[1], -1),
        qr.astype(F32), paged_s.reshape(nb, 1, -1), win_s.reshape(nb, 1, -1), gr,
        bd, w_cmp, pe_col, _overlap_matrix(past_len // CMP_STRIDE, n_selw),
        _block_onehot(past_len, LANES), 4)
    o_s = o_s.reshape(nb, GROUP, S_HPAD, HEAD_DIM)[:, :, :N_KV].transpose(0, 2, 1, 3).reshape(nb, d)
    ys = _linres(xs, o_s, w_out_b, nb)
    kv_shape = (4, N_KV, HEAD_DIM)
    return (yp, ys, paged.reshape(b, t, *kv_shape), paged_s.reshape(nb, 1, *kv_shape),
            win_p.reshape(b, win_p.shape[1], 2, N_KV, HEAD_DIM),
            win_new.reshape(nb, win_new.shape[1], 2, N_KV, HEAD_DIM))


def kernel(x_prompt, x_sample, state_conv_a, state_conv_b, cache_kv, state_kv_win, page_table,
           norm_mix, norm_ffn, a_w_in, a_b_in, a_w_dw, a_b_dw, a_ln_g, a_ln_b, a_w_out, a_b_out,
           b_w_in, b_w_dw, b_w_out, c_w_in, c_q_norm, c_k_norm, c_w_cmp, c_pe_cmp, c_w_out,
           ffn_w_gate, ffn_w_up, ffn_w_down):
    b, t, d = x_prompt.shape
    nb = x_sample.shape[0]
    depth = norm_mix.shape[0]
    xp = x_prompt
    xs = x_sample.reshape(nb, d)
    conv_a_p, conv_a_s, conv_b_p, conv_b_s = [], [], [], []
    kv_p, kv_s, win_p, win_s = [], [], [], []
    for i in range(depth):
        kind, j = i % 3, i // 3
        g = norm_mix[i]
        if kind == 0:
            wa = (g[None, :], a_w_in[j].astype(BF16), a_b_in[j][None, :], a_w_dw[j], a_b_dw[j][None, :],
                  a_ln_g[j][None, :], a_ln_b[j][None, :], a_w_out[j].astype(BF16), a_b_out[j][None, :])
            xp, st_p = _mixa_prompt(xp, *wa, 256)
            xs, st_s = _mixa_sample(xs, state_conv_a[j], *wa)
            conv_a_p.append(st_p)
            conv_a_s.append(st_s)
        elif kind == 1:
            wb = (g[None, :], b_w_in[j].astype(BF16), b_w_dw[j], b_w_out[j].astype(BF16))
            xp, st_p = _mixb_prompt(xp, *wb, 256)
            xs, st_s = _mixb_sample(xs, state_conv_b[j], *wb)
            conv_b_p.append(st_p)
            conv_b_s.append(st_s)
        else:
            xp, xs, rows_p, rows_s, wst_p, wst_s = _nsa_layer(
                xp, xs, g, cache_kv[j], state_kv_win[j], page_table, c_w_in[j], c_q_norm[j],
                c_k_norm[j], c_w_cmp[j], c_pe_cmp[j], c_w_out[j])
            kv_p.append(rows_p)
            kv_s.append(rows_s)
            win_p.append(wst_p)
            win_s.append(wst_s)
        wf = (norm_ffn[i][None, :], ffn_w_gate[i].astype(BF16), ffn_w_up[i].astype(BF16),
              ffn_w_down[i].astype(BF16))
        xp = _ffn(xp.reshape(b * t, d), *wf, 512).reshape(b, t, d)
        xs = _ffn(xs, *wf, nb)
    return (xp, xs.reshape(nb, 1, d), jnp.stack(conv_a_p), jnp.stack(conv_a_s), jnp.stack(conv_b_p),
            jnp.stack(conv_b_s), jnp.stack(kv_p), jnp.stack(kv_s), jnp.stack(win_p), jnp.stack(win_s))
```

```python
import functools
import math

import jax
import jax.numpy as jnp
from jax import lax
from jax.experimental import pallas as pl
from jax.experimental.pallas import tpu as pltpu

F32 = jnp.float32
BF16 = jnp.bfloat16

EPS = 1e-6
HEAD_DIM = 64
N_KV = 4
GROUP = 4
N_HEADS = N_KV * GROUP
ROT_DIM = HEAD_DIM // 4
ROPE_THETA = 500000.0
CMP_BLOCK = 32
CMP_STRIDE = 16
SEL_BLOCK = 64
N_SELECT = 16
WINDOW = 512
FORCE_SCORE = 1e9
MASK_BIAS = -1e9
M_INIT = -1e30

LANES = 128
SUBLANES = 8
VMEM_LIMIT = 56 * 1024 * 1024


def _cparams(n_grid):
    return pltpu.CompilerParams(dimension_semantics=("arbitrary",) * n_grid,
                                vmem_limit_bytes=VMEM_LIMIT)


def _const_spec(shape):
    nd = len(shape)
    return pl.BlockSpec(shape, lambda *_: (0,) * nd, pipeline_mode=pl.Buffered(1))


def _dot(a, b):
    return jnp.dot(a, b, preferred_element_type=F32)


def _dot_t(a, b):
    return lax.dot_general(a, b, (((1,), (1,)), ((), ())), preferred_element_type=F32)


def _rms(x, g):
    return x * lax.rsqrt(jnp.mean(x * x, axis=-1, keepdims=True) + EPS) * g


def _split3(x):
    hi = x.astype(BF16)
    r = x - hi.astype(F32)
    mid = r.astype(BF16)
    lo = (r - mid.astype(F32)).astype(BF16)
    return hi, mid, lo


def _ffn_body(x_ref, g_ref, wg_ref, wu_ref, wd_ref, o_ref, *, fc):
    x = x_ref[...]
    xn = _rms(x, g_ref[...]).astype(BF16)
    d_ff = wg_ref.shape[1]
    acc = None
    for c in range(d_ff // fc):
        hg = _dot(xn, wg_ref[:, c * fc:(c + 1) * fc])
        hu = _dot(xn, wu_ref[:, c * fc:(c + 1) * fc])
        a = (hg * jax.nn.sigmoid(hg) * hu).astype(BF16)
        d = _dot(a, wd_ref[c * fc:(c + 1) * fc, :])
        acc = d if acc is None else acc + d
    o_ref[...] = x + acc


def _ffn(x, g, wg, wu, wd, tm):
    m, d = x.shape
    d_ff = wg.shape[1]
    return pl.pallas_call(
        functools.partial(_ffn_body, fc=256),
        grid=(m // tm,),
        in_specs=[pl.BlockSpec((tm, d), lambda i: (i, 0)),
                  _const_spec((1, d)), _const_spec((d, d_ff)), _const_spec((d, d_ff)),
                  _const_spec((d_ff, d))],
        out_specs=pl.BlockSpec((tm, d), lambda i: (i, 0)),
        out_shape=jax.ShapeDtypeStruct((m, d), F32),
        compiler_params=_cparams(1), name="ffn",
    )(x, g, wg, wu, wd)


def _linres_body(x_ref, o_ref, w_ref, y_ref):
    y_ref[...] = x_ref[...] + _dot(o_ref[...], w_ref[...])


def _linres(x, o, w, tm):
    m, d = x.shape
    k = o.shape[1]
    return pl.pallas_call(
        _linres_body, grid=(m // tm,),
        in_specs=[pl.BlockSpec((tm, d), lambda i: (i, 0)),
                  pl.BlockSpec((tm, k), lambda i: (i, 0)), _const_spec((k, d))],
        out_specs=pl.BlockSpec((tm, d), lambda i: (i, 0)),
        out_shape=jax.ShapeDtypeStruct((m, d), F32),
        compiler_params=_cparams(1), name="linres",
    )(x, o, w)


A_HALO = 32
A_CHUNK = 32


def _ln_swish(c, g, b):
    mu = jnp.mean(c, axis=-1, keepdims=True)
    var = jnp.mean(jnp.square(c - mu), axis=-1, keepdims=True)
    y = (c - mu) * lax.rsqrt(var + EPS) * g + b
    return y * jax.nn.sigmoid(y)


def _mixa_prompt_body(x_ref, g_ref, win_ref, bin_ref, wdw8_ref, bdw_ref, lng_ref, lnb_ref,
                      wout_ref, bout_ref, y_ref, tail_ref, ubuf, ushift, cbuf, *, width):
    i = pl.program_id(1)
    tm, d = x_ref.shape[1], x_ref.shape[2]

    @pl.when(i == 0)
    def _():
        ubuf[0:A_HALO, :] = jnp.zeros((A_HALO, d), F32)

    x = x_ref[0]
    xn = _rms(x, g_ref[...]).astype(BF16)
    h = _dot(xn, win_ref[...]) + bin_ref[...]
    u = h[:, :d] * jax.nn.sigmoid(h[:, d:])
    ubuf[A_HALO:, :] = u
    span = tm + A_HALO - SUBLANES
    for s in range(1, SUBLANES):
        ushift[s - 1] = ubuf[s:s + span, :]
    off = A_HALO - (width - 1)
    for r0 in range(0, tm, A_CHUNK):
        acc = None
        for k in range(width):
            s = (off + k) % SUBLANES
            base = r0 + off + k - s
            rows = ubuf[base:base + A_CHUNK, :] if s == 0 else ushift[s - 1, base:base + A_CHUNK, :]
            t = jnp.concatenate([wdw8_ref[k]] * (A_CHUNK // SUBLANES), axis=0) * rows
            acc = t if acc is None else acc + t
        c = _ln_swish(acc + bdw_ref[...], lng_ref[...], lnb_ref[...])
        cbuf[r0:r0 + A_CHUNK, :] = c.astype(BF16)
    y_ref[0] = x + _dot(cbuf[...], wout_ref[...]) + bout_ref[...]
    last = ubuf[tm:tm + A_HALO, :]
    ubuf[0:A_HALO, :] = last

    @pl.when(i == pl.num_programs(1) - 1)
    def _():
        tail_ref[0] = last


def _mixa_prompt(x, g, w_in, b_in, w_dw, b_dw, ln_g, ln_b, w_out, b_out, tm):
    b, t, d = x.shape
    width = w_dw.shape[0]
    y, tail = pl.pallas_call(
        functools.partial(_mixa_prompt_body, width=width),
        grid=(b, t // tm),
        in_specs=[pl.BlockSpec((1, tm, d), lambda bi, i: (bi, i, 0)),
                  _const_spec((1, d)), _const_spec((d, 2 * d)), _const_spec((1, 2 * d)),
                  _const_spec((width, SUBLANES, d)), _const_spec((1, d)), _const_spec((1, d)),
                  _const_spec((1, d)), _const_spec((d, d)), _const_spec((1, d))],
        out_specs=[pl.BlockSpec((1, tm, d), lambda bi, i: (bi, i, 0)),
                   pl.BlockSpec((1, A_HALO, d), lambda bi, i: (bi, 0, 0))],
        out_shape=[jax.ShapeDtypeStruct((b, t, d), F32),
                   jax.ShapeDtypeStruct((b, A_HALO, d), F32)],
        scratch_shapes=[pltpu.VMEM((tm + A_HALO, d), F32),
                        pltpu.VMEM((SUBLANES - 1, tm + A_HALO - SUBLANES, d), F32),
                        pltpu.VMEM((tm, d), BF16)],
        compiler_params=_cparams(2), name="mixa_prompt",
    )(x, g, w_in, b_in, jnp.broadcast_to(w_dw[:, None, :], (width, SUBLANES, d)), b_dw, ln_g, ln_b,
      w_out, b_out)
    return y, tail[:, A_HALO - (width - 1):]


def _mixa_sample_body(x_ref, hist_ref, g_ref, win_ref, bin_ref, wdw_ref, bdw_ref, lng_ref,
                      lnb_ref, wout_ref, bout_ref, y_ref, st_ref, *, width):
    d = x_ref.shape[1]
    x = x_ref[...]
    xn = _rms(x, g_ref[...]).astype(BF16)
    h = _dot(xn, win_ref[...]) + bin_ref[...]
    u = h[:, :d] * jax.nn.sigmoid(h[:, d:])
    acc = None
    for k in range(width - 1):
        row = hist_ref[:, k, :]
        t = wdw_ref[k:k + 1, :] * row
        acc = t if acc is None else acc + t
        if k >= 1:
            st_ref[:, k - 1, :] = row
    acc = acc + wdw_ref[width - 1:width, :] * u
    st_ref[:, width - 2, :] = u
    c = _ln_swish(acc + bdw_ref[...], lng_ref[...], lnb_ref[...])
    y_ref[...] = x + _dot(c.astype(BF16), wout_ref[...]) + bout_ref[...]


def _mixa_sample(x, hist, g, w_in, b_in, w_dw, b_dw, ln_g, ln_b, w_out, b_out):
    n, d = x.shape
    width = w_dw.shape[0]
    return pl.pallas_call(
        functools.partial(_mixa_sample_body, width=width),
        out_shape=[jax.ShapeDtypeStruct((n, d), F32),
                   jax.ShapeDtypeStruct((n, width - 1, d), F32)],
        compiler_params=pltpu.CompilerParams(vmem_limit_bytes=VMEM_LIMIT), name="mixa_sample",
    )(x, hist, g, w_in, b_in, w_dw, b_dw, ln_g, ln_b, w_out, b_out)


B_HALO = 8


def _mixb_prompt_body(x_ref, g_ref, win_ref, wdw_ref, wout_ref, y_ref, tail_ref, cbuf, *, width):
    i = pl.program_id(1)
    tm, d = x_ref.shape[1], x_ref.shape[2]

    @pl.when(i == 0)
    def _():
        cbuf[0:B_HALO, :] = jnp.zeros((B_HALO, d), F32)

    x = x_ref[0]
    xn = _rms(x, g_ref[...]).astype(BF16)
    h = _dot(xn, win_ref[...])
    cbuf[B_HALO:, :] = h[:, d:2 * d] * h[:, 2 * d:]
    off = B_HALO - (width - 1)
    conv = None
    for k in range(width):
        t = wdw_ref[k:k + 1, :] * cbuf[off + k:off + k + tm, :]
        conv = t if conv is None else conv + t
    z = (h[:, :d] * conv).astype(BF16)
    y_ref[0] = x + _dot(z, wout_ref[...])
    last = cbuf[tm:tm + B_HALO, :]
    cbuf[0:B_HALO, :] = last

    @pl.when(i == pl.num_programs(1) - 1)
    def _():
        tail_ref[0] = last


def _mixb_prompt(x, g, w_in, w_dw, w_out, tm):
    b, t, d = x.shape
    width = w_dw.shape[0]
    y, tail = pl.pallas_call(
        functools.partial(_mixb_prompt_body, width=width),
        grid=(b, t // tm),
        in_specs=[pl.BlockSpec((1, tm, d), lambda bi, i: (bi, i, 0)),
                  _const_spec((1, d)), _const_spec((d, 3 * d)), _const_spec((width, d)),
                  _const_spec((d, d))],
        out_specs=[pl.BlockSpec((1, tm, d), lambda bi, i: (bi, i, 0)),
                   pl.BlockSpec((1, B_HALO, d), lambda bi, i: (bi, 0, 0))],
        out_shape=[jax.ShapeDtypeStruct((b, t, d), F32),
                   jax.ShapeDtypeStruct((b, B_HALO, d), F32)],
        scratch_shapes=[pltpu.VMEM((tm + B_HALO, d), F32)],
        compiler_params=_cparams(2), name="mixb_prompt",
    )(x, g, w_in, w_dw, w_out)
    return y, tail[:, B_HALO - (width - 1):]


def _mixb_sample_body(x_ref, hist_ref, g_ref, win_ref, wdw_ref, wout_ref, y_ref, st_ref, *, width):
    d = x_ref.shape[1]
    x = x_ref[...]
    xn = _rms(x, g_ref[...]).astype(BF16)
    h = _dot(xn, win_ref[...])
    cv = h[:, d:2 * d] * h[:, 2 * d:]
    conv = None
    for k in range(width - 1):
        row = hist_ref[:, k, :]
        t = wdw_ref[k:k + 1, :] * row
        conv = t if conv is None else conv + t
        if k >= 1:
            st_ref[:, k - 1, :] = row
    conv = conv + wdw_ref[width - 1:width, :] * cv
    st_ref[:, width - 2, :] = cv
    z = (h[:, :d] * conv).astype(BF16)
    y_ref[...] = x + _dot(z, wout_ref[...])


def _mixb_sample(x, hist, g, w_in, w_dw, w_out):
    n, d = x.shape
    width = w_dw.shape[0]
    return pl.pallas_call(
        functools.partial(_mixb_sample_body, width=width),
        out_shape=[jax.ShapeDtypeStruct((n, d), F32),
                   jax.ShapeDtypeStruct((n, width - 1, d), F32)],
        compiler_params=pltpu.CompilerParams(vmem_limit_bytes=VMEM_LIMIT), name="mixb_sample",
    )(x, hist, g, w_in, w_dw, w_out)


def _rope_tables(pos):
    half = ROT_DIM // 2
    inv = jnp.exp(jnp.arange(half, dtype=F32) * (-2.0 * math.log(ROPE_THETA) / ROT_DIM))
    ang = pos.astype(F32)[:, None] * inv[None, :]
    cos, sin = jnp.cos(ang), jnp.sin(ang)
    r = jnp.arange(LANES) % HEAD_DIM
    c = jnp.where(r[None, :] < ROT_DIM, cos[:, r % half], 1.0)
    s1 = jnp.where(r[None, :] < half, -sin[:, r % half], 0.0)
    s2 = jnp.where((r[None, :] >= half) & (r[None, :] < ROT_DIM), sin[:, r % half], 0.0)
    return c, s1, s2


def _head_norm_rope(blk, gvec, c, s1, s2):
    lane = lax.broadcasted_iota(jnp.int32, (1, LANES), 1)
    first = lane < HEAD_DIM
    x2 = blk * blk
    ms_a = jnp.sum(jnp.where(first, x2, 0.0), axis=-1, keepdims=True)
    ms_b = jnp.sum(jnp.where(first, 0.0, x2), axis=-1, keepdims=True)
    ms = jnp.where(first, ms_a, ms_b) * (1.0 / HEAD_DIM)
    y = blk * lax.rsqrt(ms + EPS) * gvec
    return y * c + pltpu.roll(y, LANES - ROT_DIM // 2, 1) * s1 + pltpu.roll(y, ROT_DIM // 2, 1) * s2


GATE_ROWS = 16


def _nsa_proj_body(x_ref, g_ref, wqkv_ref, wg_ref, qn_ref, kn_ref, c_ref, s1_ref, s2_ref, *outs,
                   prompt_layout):
    if prompt_layout:
        qt_ref, paged_ref, win_ref, cmp_ref, kaug_ref, vst_ref, kw_ref, vwt_ref, gt_ref = outs
    else:
        q_ref, paged_ref, win_ref, gate_ref = outs
    x = x_ref[0]
    tm = x.shape[0]
    xn = _rms(x, g_ref[...]).astype(BF16)
    h = _dot(xn, wqkv_ref[...])
    gh = _dot(xn, wg_ref[...])
    c, s1, s2 = c_ref[...], s1_ref[...], s2_ref[...]
    nq = N_HEADS * HEAD_DIM
    kvw = N_KV * HEAD_DIM
    scale = HEAD_DIM ** -0.5

    def put_heads_t(dst, pair, blk_t):
        for j in range(tm // LANES):
            dst[0, 2 * pair, j] = blk_t[:HEAD_DIM, j * LANES:(j + 1) * LANES]
            dst[0, 2 * pair + 1, j] = blk_t[HEAD_DIM:, j * LANES:(j + 1) * LANES]

    for cb in range(nq // LANES):
        blk = _head_norm_rope(h[:, cb * LANES:(cb + 1) * LANES], qn_ref[...], c, s1, s2) * scale
        if prompt_layout:
            bt = blk.T.astype(BF16)
            qt_ref[0, 2 * cb] = bt[:HEAD_DIM]
            qt_ref[0, 2 * cb + 1] = bt[HEAD_DIM:]
        else:
            q_ref[0, :, cb * LANES:(cb + 1) * LANES] = blk.astype(BF16)
    if prompt_layout:
        t_abs = pl.program_id(1) * tm + lax.broadcasted_iota(jnp.int32, (tm, 1), 0)
        lane = lax.broadcasted_iota(jnp.int32, (1, LANES), 1)
        onehot = jnp.where(t_abs // SEL_BLOCK == lane, 1.0, 0.0).astype(BF16)
    for br in range(3):
        kbase = nq + br * 2 * kvw
        for cb in range(kvw // LANES):
            kblk = _head_norm_rope(h[:, kbase + cb * LANES:kbase + (cb + 1) * LANES],
                                   kn_ref[br:br + 1, :], c, s1, s2)
            vblk = h[:, kbase + kvw + cb * LANES:kbase + kvw + (cb + 1) * LANES]
            if br < 2:
                paged_ref[0, :, br * 2 * kvw + cb * LANES:br * 2 * kvw + (cb + 1) * LANES] = kblk
                paged_ref[0, :, br * 2 * kvw + kvw + cb * LANES:
                          br * 2 * kvw + kvw + (cb + 1) * LANES] = vblk
            else:
                win_ref[0, :, cb * LANES:(cb + 1) * LANES] = kblk
                win_ref[0, :, kvw + cb * LANES:kvw + (cb + 1) * LANES] = vblk
            if not prompt_layout:
                continue
            if br == 0:
                cmp_ref[0, cb] = kblk
                cmp_ref[0, kvw // LANES + cb] = vblk
            elif br == 1:
                kaug_ref[0, 2 * cb] = jnp.concatenate([onehot, kblk.astype(BF16)], axis=1)
                kaug_ref[0, 2 * cb + 1] = jnp.concatenate(
                    [onehot, pltpu.roll(kblk, HEAD_DIM, 1).astype(BF16)], axis=1)
                put_heads_t(vst_ref, cb, vblk.T.astype(BF16))
            else:
                kb = kblk.astype(BF16)
                kw_ref[0, 2 * cb] = kb[:, :HEAD_DIM]
                kw_ref[0, 2 * cb + 1] = kb[:, HEAD_DIM:]
                put_heads_t(vwt_ref, cb, vblk.T.astype(BF16))
    if prompt_layout:
        for hh in range(N_KV):
            gt_ref[0, hh] = gh[:, hh * LANES:(hh + 1) * LANES].T[:GATE_ROWS]
    else:
        gate_ref[0] = gh


def _nsa_proj(x, g, wqkv, wgate, qn, kn, rope, tm, prompt_layout):
    b, t, d = x.shape
    c, s1, s2 = rope
    nqkv = wqkv.shape[1]
    kvw = N_KV * HEAD_DIM
    row = lambda w, dt: (jax.ShapeDtypeStruct((b, t, w), dt),
                         pl.BlockSpec((1, tm, w), lambda bi, i: (bi, i, 0)))
    tab_spec = pl.BlockSpec((tm, LANES), lambda bi, i: (i, 0))
    if prompt_layout:
        vt = (jax.ShapeDtypeStruct((b, N_KV, t // LANES, HEAD_DIM, LANES), BF16),
              pl.BlockSpec((1, N_KV, tm // LANES, HEAD_DIM, LANES), lambda bi, i: (bi, 0, i, 0, 0)))
        outs = [(jax.ShapeDtypeStruct((b, N_HEADS, HEAD_DIM, t), BF16),
                 pl.BlockSpec((1, N_HEADS, HEAD_DIM, tm), lambda bi, i: (bi, 0, 0, i))),
                row(4 * kvw, F32), row(2 * kvw, F32),
                (jax.ShapeDtypeStruct((b, 4, t, LANES), F32),
                 pl.BlockSpec((1, 4, tm, LANES), lambda bi, i: (bi, 0, i, 0))),
                (jax.ShapeDtypeStruct((b, N_KV, t, 2 * LANES), BF16),
                 pl.BlockSpec((1, N_KV, tm, 2 * LANES), lambda bi, i: (bi, 0, i, 0))),
                vt,
                (jax.ShapeDtypeStruct((b, N_KV, t, HEAD_DIM), BF16),
                 pl.BlockSpec((1, N_KV, tm, HEAD_DIM), lambda bi, i: (bi, 0, i, 0))),
                vt,
                (jax.ShapeDtypeStruct((b, N_KV, GATE_ROWS, t), F32),
                 pl.BlockSpec((1, N_KV, GATE_ROWS, tm), lambda bi, i: (bi, 0, 0, i)))]
    else:
        outs = [row(N_HEADS * HEAD_DIM, BF16), row(4 * kvw, F32), row(2 * kvw, F32),
                row(N_KV * LANES, F32)]
    return pl.pallas_call(
        functools.partial(_nsa_proj_body, prompt_layout=prompt_layout), grid=(b, t // tm),
        in_specs=[pl.BlockSpec((1, tm, d), lambda bi, i: (bi, i, 0)), _const_spec((1, d)),
                  _const_spec((d, nqkv)), _const_spec((d, N_KV * LANES)), _const_spec((1, LANES)),
                  _const_spec((3, LANES)), tab_spec, tab_spec, tab_spec],
        out_specs=[o[1] for o in outs], out_shape=[o[0] for o in outs],
        compiler_params=_cparams(2), name="nsa_proj" if prompt_layout else "nsa_proj_sample",
    )(x, g, wqkv, wgate, qn, kn, c, s1, s2)


def _cmp_const(w_ref, pe_ref, kv):
    cst = jnp.sum(w_ref[kv] * pe_ref[kv], axis=0, keepdims=True)
    return jnp.concatenate([cst] * N_KV, axis=1)


SHIFT_PAD = 8


def _compress_rows(load, bd_ref, kv, n_blk, shift_buf):
    first = second = None
    for l in range(CMP_STRIDE):
        xl = jnp.concatenate([load(0, l), load(1, l)], axis=1).astype(BF16)
        ta = _dot(xl, bd_ref[kv, l])
        tb = _dot(xl, bd_ref[kv, CMP_STRIDE + l])
        first = ta if first is None else first + ta
        second = tb if second is None else second + tb
    shift_buf[0:n_blk, :] = second
    shift_buf[n_blk:n_blk + SHIFT_PAD, :] = jnp.zeros((SHIFT_PAD, shift_buf.shape[1]), F32)
    return first + shift_buf[1:n_blk + 1, :]


def _compress_body(x_ref, bd_ref, w_ref, pe_ref, kc_ref, vct_ref, shift_buf):
    kv = pl.program_id(1)
    n_blk = x_ref.shape[2] // CMP_STRIDE
    out = _compress_rows(lambda c, l: x_ref[0, c, pl.ds(l, n_blk, stride=CMP_STRIDE), :],
                         bd_ref, kv, n_blk, shift_buf) + _cmp_const(w_ref, pe_ref, kv)

    @pl.when(kv == 0)
    def _():
        for hh in range(N_KV):
            kc_ref[0, hh] = out[:, hh * HEAD_DIM:(hh + 1) * HEAD_DIM].astype(BF16)

    @pl.when(kv == 1)
    def _():
        out_t = out.T.astype(BF16)
        for hh in range(N_KV):
            vct_ref[0, hh] = out_t[hh * HEAD_DIM:(hh + 1) * HEAD_DIM]


def _compress(cmp_rows, bd, w_cmp, pe_col):
    b, _, t, _ = cmp_rows.shape
    n_blk = t // CMP_STRIDE
    return pl.pallas_call(
        _compress_body, grid=(b, 2),
        in_specs=[pl.BlockSpec((1, 2, t, LANES), lambda bi, kv: (bi, kv, 0, 0)),
                  _const_spec(bd.shape), _const_spec(w_cmp.shape), _const_spec(pe_col.shape)],
        out_specs=[pl.BlockSpec((1, N_KV, n_blk, HEAD_DIM), lambda bi, kv: (bi, 0, 0, 0)),
                   pl.BlockSpec((1, N_KV, HEAD_DIM, n_blk), lambda bi, kv: (bi, 0, 0, 0))],
        out_shape=[jax.ShapeDtypeStruct((b, N_KV, n_blk, HEAD_DIM), BF16),
                   jax.ShapeDtypeStruct((b, N_KV, HEAD_DIM, n_blk), BF16)],
        scratch_shapes=[pltpu.VMEM((n_blk + SHIFT_PAD, N_KV * HEAD_DIM), F32)],
        compiler_params=_cparams(2), name="nsa_compress",
    )(cmp_rows, bd, w_cmp, pe_col)


def _softmax(s, mask, axis):
    s = jnp.where(mask, s, -jnp.inf)
    m = jnp.max(s, axis=axis, keepdims=True)
    m = jnp.where(m == -jnp.inf, 0.0, m)
    e = jnp.exp(s - m)
    return e / jnp.maximum(jnp.sum(e, axis=axis, keepdims=True), 1e-30)


def _importance(p_sum, ov_ref, axis):
    hi, mid, lo = _split3(p_sum)
    ov = ov_ref[...]
    if axis == 0:
        return _dot(ov, hi) + _dot(ov, mid) + _dot(ov, lo)
    return _dot(hi, ov) + _dot(mid, ov) + _dot(lo, ov)


def _select_blocks(imp, q_pos, n_sel, axis):
    w = imp.shape[axis]
    j = lax.broadcasted_iota(jnp.int32, imp.shape, axis)
    cur = q_pos // SEL_BLOCK
    forced = (j == 0) | (j == cur) | (j == cur - 1)
    v = jnp.where(forced, FORCE_SCORE, imp)
    v = jnp.where((j * SEL_BLOCK > q_pos) | (j >= n_sel), -jnp.inf, v)
    sel = jnp.zeros(imp.shape, jnp.bool_)
    for _ in range(min(N_SELECT, n_sel)):
        mx = jnp.max(v, axis=axis, keepdims=True)
        idx = jnp.min(jnp.where(v == mx, j, w), axis=axis, keepdims=True)
        hit = (j == idx) & (mx > -jnp.inf)
        sel = sel | hit
        v = jnp.where(hit, -jnp.inf, v)
    return sel


def _nsa_attn_body(qt_ref, kc_ref, vct_ref, kaug_ref, vst_ref, kw_ref, vwt_ref, gt_ref, ovt_ref,
                   o_ref, *, tq, tk):
    t0 = pl.program_id(2) * tq
    cols = GROUP * tq
    n_cmp = kc_ref.shape[2]
    t_len = kaug_ref.shape[2]
    qt = jnp.concatenate([qt_ref[0, g] for g in range(GROUP)], axis=1)
    col_t = t0 + lax.rem(lax.broadcasted_iota(jnp.int32, (1, cols), 1), tq)

    ci = lax.broadcasted_iota(jnp.int32, (n_cmp, 1), 0)
    p_c = _softmax(_dot(kc_ref[0, 0], qt), ci * CMP_STRIDE + (CMP_BLOCK - 1) <= col_t, 0)
    o_c = _dot(vct_ref[0, 0], p_c.astype(BF16))
    p_sum = p_c[:, 0:tq]
    for g in range(1, GROUP):
        p_sum = p_sum + p_c[:, g * tq:(g + 1) * tq]

    q_pos = t0 + lax.broadcasted_iota(jnp.int32, (1, tq), 1)
    sel = _select_blocks(_importance(p_sum, ovt_ref, 0), q_pos, t_len // SEL_BLOCK, 0)
    bias = jnp.where(sel, 0.0, MASK_BIAS).astype(BF16)
    rhs = jnp.concatenate([jnp.concatenate([bias] * GROUP, axis=1), qt,
                           jnp.zeros((LANES - HEAD_DIM, cols), BF16)], axis=0)

    vt_blocks = tk // LANES

    def sel_tile(kt, carry, causal):
        m, l, acc = carry
        k0 = pl.multiple_of(kt * tk, tk)
        s = _dot(kaug_ref[0, 0, pl.ds(k0, tk), :], rhs)
        if causal:
            kpos = k0 + lax.broadcasted_iota(jnp.int32, (tk, 1), 0)
            s = jnp.where(kpos <= col_t, s, MASK_BIAS)
        m_new = jnp.maximum(m, jnp.max(s, axis=0, keepdims=True))
        alpha = jnp.exp(m - m_new)
        p = jnp.exp(s - m_new)
        l = alpha * l + jnp.sum(p, axis=0, keepdims=True)
        vt = jnp.concatenate([vst_ref[0, 0, kt * vt_blocks + j] for j in range(vt_blocks)], axis=1)
        acc = alpha * acc + _dot(vt, p.astype(BF16))
        return m_new, l, acc

    kt_diag = t0 // tk
    carry = (jnp.full((1, cols), M_INIT, F32), jnp.zeros((1, cols), F32),
             jnp.zeros((HEAD_DIM, cols), F32))
    carry = lax.fori_loop(0, kt_diag, functools.partial(sel_tile, causal=False), carry)
    _, l_s, acc_s = sel_tile(kt_diag, carry, True)
    o_s = acc_s / l_s

    nw = WINDOW + tq
    w0 = pl.multiple_of(jnp.maximum(t0 - WINDOW, 0), tq)
    kpos = w0 + lax.broadcasted_iota(jnp.int32, (nw, 1), 0)
    p_w = _softmax(_dot(kw_ref[0, 0, pl.ds(w0, nw), :], qt),
                   (kpos <= col_t) & (kpos >= col_t - WINDOW), 0)
    vwt = jnp.concatenate([vwt_ref[0, 0, w0 // LANES + j] for j in range(nw // LANES)], axis=1)
    o_w = _dot(vwt, p_w.astype(BF16))

    gate = jax.nn.sigmoid(gt_ref[0, 0])
    outs = []
    for g in range(GROUP):
        sl = slice(g * tq, (g + 1) * tq)
        outs.append(gate[3 * g:3 * g + 1] * o_c[:, sl] + gate[3 * g + 1:3 * g + 2] * o_s[:, sl]
                    + gate[3 * g + 2:3 * g + 3] * o_w[:, sl])
    o_ref[0] = jnp.concatenate(outs, axis=0).T.astype(BF16)


def _nsa_attn(qt, kc, vct, kaug, vst, kw, vwt, gt, overlap_t, tq, tk):
    b, _, _, t = qt.shape
    n_cmp = kc.shape[2]
    assert t // SEL_BLOCK <= LANES and tq % LANES == 0 and tk % tq == 0 and t % tk == 0
    per_head = lambda *shape: pl.BlockSpec((1, 1) + shape, lambda bi, h, i: (bi, h) + (0,) * len(shape))
    return pl.pallas_call(
        functools.partial(_nsa_attn_body, tq=tq, tk=tk),
        grid=(b, N_KV, t // tq),
        in_specs=[pl.BlockSpec((1, GROUP, HEAD_DIM, tq), lambda bi, h, i: (bi, h, 0, i)),
                  per_head(n_cmp, HEAD_DIM), per_head(HEAD_DIM, n_cmp), per_head(t, 2 * LANES),
                  per_head(t // LANES, HEAD_DIM, LANES), per_head(t, HEAD_DIM),
                  per_head(t // LANES, HEAD_DIM, LANES),
                  pl.BlockSpec((1, 1, GATE_ROWS, tq), lambda bi, h, i: (bi, h, 0, i)),
                  _const_spec(overlap_t.shape)],
        out_specs=pl.BlockSpec((1, tq, GROUP * HEAD_DIM), lambda bi, h, i: (bi, i, h)),
        out_shape=jax.ShapeDtypeStruct((b, t, N_HEADS * HEAD_DIM), BF16),
        compiler_params=_cparams(3), name="nsa_attn",
    )(qt, kc, vct, kaug, vst, kw, vwt, gt, overlap_t)


S_ROWS = 32
S_HPAD = 8


def _nsa_sample_body(pt_ref, *refs, n_pg, past_len):
    pages = refs[:n_pg]
    (win_ref, q_ref, new_ref, wnew_ref, g_ref, bd_ref, w_ref, pe_ref, ov_ref, e_ref,
     o_ref, wout_ref, kcbuf, vcbuf, ksbuf, vsbuf, shift_buf) = refs[n_pg:]
    step = pl.program_id(1)
    kvw = N_KV * HEAD_DIM
    psz = pages[0].shape[1]
    for i in range(n_pg):
        r0 = pl.multiple_of((step * n_pg + i) * psz, psz)
        for c in range(2):
            kcbuf[c, pl.ds(r0, psz), :] = pages[i][0, :, c * LANES:(c + 1) * LANES]
            vcbuf[c, pl.ds(r0, psz), :] = pages[i][0, :, kvw + c * LANES:kvw + (c + 1) * LANES]
        ksbuf[pl.ds(r0, psz), :] = pages[i][0, :, 2 * kvw:3 * kvw].astype(BF16)
        vsbuf[pl.ds(r0, psz), :] = pages[i][0, :, 3 * kvw:4 * kvw].astype(BF16)

    @pl.when(step == pl.num_programs(1) - 1)
    def _():
        n_blk = past_len // CMP_STRIDE
        new = new_ref[0]
        cmp_of = lambda buf, kv: (_compress_rows(
            lambda c, l: buf[c, pl.ds(l, n_blk, stride=CMP_STRIDE), :], bd_ref, kv, n_blk, shift_buf)
            + _cmp_const(w_ref, pe_ref, kv)).astype(BF16)
        kc = cmp_of(kcbuf, 0)
        vc = cmp_of(vcbuf, 1)

        row = lax.broadcasted_iota(jnp.int32, (S_ROWS, kvw), 0)
        lane = lax.broadcasted_iota(jnp.int32, (S_ROWS, kvw), 1)
        own = lane // HEAD_DIM == lax.rem(row, S_HPAD)
        q = jnp.where(own, jnp.concatenate([q_ref[0]] * N_KV, axis=1), 0.0).astype(BF16)
        qf = q.astype(F32)

        def own_head(full):
            z = jnp.where(own, full, 0.0)
            out = z[:, 0:HEAD_DIM]
            for hh in range(1, N_KV):
                out = out + z[:, hh * HEAD_DIM:(hh + 1) * HEAD_DIM]
            return out

        def score_new(k_row):
            return jnp.sum(qf * k_row.astype(BF16).astype(F32), axis=-1, keepdims=True)

        def attend(s, mask, v, s_new, v_row):
            s = jnp.where(mask, s, -jnp.inf)
            m = jnp.maximum(jnp.max(s, axis=-1, keepdims=True), s_new)
            e = jnp.exp(s - m)
            e_new = jnp.exp(s_new - m)
            den = jnp.sum(e, axis=-1, keepdims=True) + e_new
            p = e / den
            p_new = (e_new / den).astype(BF16).astype(F32)
            return own_head(_dot(p.astype(BF16), v) + p_new * v_row.astype(BF16).astype(F32))

        q_pos = past_len
        ci = lax.broadcasted_iota(jnp.int32, (1, n_blk), 1)
        p_c = _softmax(_dot_t(q, kc), ci * CMP_STRIDE + (CMP_BLOCK - 1) <= q_pos, 1)
        o_c = own_head(_dot(p_c.astype(BF16), vc))
        p_sum = p_c[0:S_HPAD]
        for g in range(1, GROUP):
            p_sum = p_sum + p_c[g * S_HPAD:(g + 1) * S_HPAD]

        n_sel = past_len // SEL_BLOCK + 1
        sel = _select_blocks(_importance(p_sum, ov_ref, 1), jnp.full((S_HPAD, 1), q_pos, jnp.int32),
                             n_sel, 1)
        bias = jnp.where(sel, 0.0, MASK_BIAS).astype(BF16)
        bias = jnp.concatenate([bias] * GROUP, axis=0)

        s_s = _dot_t(q, ksbuf[...]) + _dot_t(bias[:, 0:e_ref.shape[1]], e_ref[...])
        o_s = attend(s_s, s_s > 0.5 * MASK_BIAS, vsbuf[...],
                     score_new(new[:, 2 * kvw:3 * kvw]), new[:, 3 * kvw:4 * kvw])

        wnew = wnew_ref[0]
        kw = win_ref[0, :, 0:kvw].astype(BF16)
        vw = win_ref[0, :, kvw:2 * kvw].astype(BF16)
        s_w = _dot_t(q, kw)
        o_w = attend(s_w, s_w == s_w, vw, score_new(wnew[:, 0:kvw]), wnew[:, kvw:2 * kvw])

        gate = jax.nn.sigmoid(g_ref[0])
        o_ref[0] = (gate[:, 0:1] * o_c + gate[:, 1:2] * o_s + gate[:, 2:3] * o_w).astype(BF16)

        wb = win_ref.shape[1]
        wout_ref[0, 0:wb - 1, :] = win_ref[0, 1:wb, :]
        wout_ref[0, wb - 1:wb, :] = wnew


def _nsa_sample(page_table, cache, win_buf, q_rows, new_rows, wnew_rows, gate_rows,
                bd, w_cmp, pe_col, overlap, onehot, n_pg):
    nb, n_pages = page_table.shape
    _, psz, roww = cache.shape
    past_len = n_pages * psz
    kvw = N_KV * HEAD_DIM
    wb = win_buf.shape[1]

    def page_spec(i):
        return pl.BlockSpec((1, psz, roww), lambda b, s, pt: (pt[b, s * n_pg + i], 0, 0))

    per_b = lambda shape: pl.BlockSpec((1,) + shape, lambda b, s, pt: (b, 0, 0))
    cst = lambda a: pl.BlockSpec(a.shape, lambda b, s, pt: (0,) * a.ndim, pipeline_mode=pl.Buffered(1))
    grid_spec = pltpu.PrefetchScalarGridSpec(
        num_scalar_prefetch=1, grid=(nb, n_pages // n_pg),
        in_specs=[page_spec(i) for i in range(n_pg)] + [
            per_b((wb, 2 * kvw)), per_b((S_ROWS, HEAD_DIM)), per_b((1, roww)), per_b((1, 2 * kvw)),
            per_b((S_ROWS, LANES)), cst(bd), cst(w_cmp), cst(pe_col), cst(overlap), cst(onehot)],
        out_specs=[per_b((S_ROWS, HEAD_DIM)), per_b((wb, 2 * kvw))],
        scratch_shapes=[pltpu.VMEM((2, past_len, LANES), F32), pltpu.VMEM((2, past_len, LANES), F32),
                        pltpu.VMEM((past_len, kvw), BF16), pltpu.VMEM((past_len, kvw), BF16),
                        pltpu.VMEM((past_len // CMP_STRIDE + SHIFT_PAD, kvw), F32)])
    return pl.pallas_call(
        functools.partial(_nsa_sample_body, n_pg=n_pg, past_len=past_len),
        grid_spec=grid_spec,
        out_shape=[jax.ShapeDtypeStruct((nb, S_ROWS, HEAD_DIM), BF16),
                   jax.ShapeDtypeStruct((nb, wb, 2 * kvw), F32)],
        compiler_params=_cparams(2), name="nsa_sample",
    )(page_table, *([cache] * n_pg), win_buf, q_rows, new_rows, wnew_rows, gate_rows,
      bd, w_cmp, pe_col, overlap, onehot)


def _overlap_matrix(n_cmp, width):
    ci = jnp.arange(n_cmp)[:, None] * CMP_STRIDE
    start = jnp.arange(width)[None, :] * SEL_BLOCK
    return ((ci < start + SEL_BLOCK) & (ci + CMP_BLOCK > start)).astype(BF16)


def _block_onehot(t, width):
    return (jnp.arange(t)[:, None] // SEL_BLOCK == jnp.arange(width)[None, :]).astype(BF16)


def _nsa_weights(w_in, q_norm, k_norm, w_cmp, pe_cmp):
    nq = N_HEADS * HEAD_DIM
    nkv = 6 * N_KV * HEAD_DIM
    d = w_in.shape[0]
    wqkv = w_in[:, :nq + nkv].astype(BF16)
    wg = w_in[:, nq + nkv:].reshape(d, N_KV, GROUP * 3)
    wg = jnp.pad(wg, ((0, 0), (0, 0), (0, LANES - GROUP * 3))).reshape(d, N_KV * LANES).astype(BF16)
    qn = jnp.tile(q_norm, LANES // HEAD_DIM)[None, :]
    kn = jnp.tile(k_norm, (1, LANES // HEAD_DIM))
    w4 = w_cmp.reshape(2, CMP_BLOCK, HEAD_DIM, HEAD_DIM)
    eye = jnp.eye(N_KV, dtype=w_cmp.dtype)
    bd = (eye[None, None, :, None, :, None] * w4[:, :, None, :, None, :]).reshape(
        2, CMP_BLOCK, N_KV * HEAD_DIM, N_KV * HEAD_DIM).astype(BF16)
    pe_col = pe_cmp.reshape(2, CMP_BLOCK * HEAD_DIM, 1)
    return wqkv, wg, qn, kn, bd, pe_col


def _nsa_layer(xp, xs, g, cache_l, win_buf, page_table, w_in, q_norm, k_norm, w_cmp, pe_cmp, w_out):
    b, t, d = xp.shape
    nb = xs.shape[0]
    wqkv, wg, qn, kn, bd, pe_col = _nsa_weights(w_in, q_norm, k_norm, w_cmp, pe_cmp)
    w_out_b = w_out.astype(BF16)
    g2 = g[None, :]

    rope_p = _rope_tables(jnp.arange(t, dtype=jnp.int32))
    qt, paged, win, cmp_rows, kaug, vst, kw, vwt, gt = _nsa_proj(
        xp, g2, wqkv, wg, qn, kn, rope_p, 256, True)
    kc, vct = _compress(cmp_rows, bd, w_cmp, pe_col)
    o = _nsa_attn(qt, kc, vct, kaug, vst, kw, vwt, gt,
                  _overlap_matrix(t // CMP_STRIDE, LANES).T, 512, 512)
    yp = _linres(xp.reshape(b * t, d), o.reshape(b * t, d), w_out_b, 512).reshape(b, t, d)
    win_p = win[:, t - min(WINDOW, t):]

    n_pages = page_table.shape[1]
    psz = cache_l.shape[1]
    past_len = n_pages * psz
    rope_s = _rope_tables(jnp.full((nb,), past_len, jnp.int32))
    q_s, paged_s, win_s, gates_s = _nsa_proj(xs[None], g2, wqkv, wg, qn, kn, rope_s, nb, False)
    qr = q_s.reshape(nb, N_KV, GROUP, HEAD_DIM).transpose(0, 2, 1, 3)
    qr = jnp.pad(qr, ((0, 0), (0, 0), (0, S_HPAD - N_KV), (0, 0))).reshape(nb, S_ROWS, HEAD_DIM)
    gr = gates_s.reshape(nb, N_KV, LANES)[:, :, :GROUP * 3].reshape(nb, N_KV, GROUP, 3)
    gr = gr.transpose(0, 2, 1, 3)
    gr = jnp.pad(gr, ((0, 0), (0, 0), (0, S_HPAD - N_KV), (0, LANES - 3))).reshape(nb, S_ROWS, LANES)
    n_selw = 2 * LANES
    o_s, win_new = _nsa_sample(
        page_table, cache_l.reshape(cache_l.shape[0], psz, -1), win_buf.reshape(nb, win_buf.shape[1], -1),
        qr.astype(F32), paged_s.reshape(nb, 1, -1), win_s.reshape(nb, 1, -1), gr,
        bd, w_cmp, pe_col, _overlap_matrix(past_len // CMP_STRIDE, n_selw),
        _block_onehot(past_len, LANES), 4)
    o_s = o_s.reshape(nb, GROUP, S_HPAD, HEAD_DIM)[:, :, :N_KV].transpose(0, 2, 1, 3).reshape(nb, d)
    ys = _linres(xs, o_s, w_out_b, nb)
    kv_shape = (4, N_KV, HEAD_DIM)
    return (yp, ys, paged.reshape(b, t, *kv_shape), paged_s.reshape(nb, 1, *kv_shape),
            win_p.reshape(b, win_p.shape[1], 2, N_KV, HEAD_DIM),
            win_new.reshape(nb, win_new.shape[1], 2, N_KV, HEAD_DIM))


def kernel(x_prompt, x_sample, state_conv_a, state_conv_b, cache_kv, state_kv_win, page_table,
           norm_mix, norm_ffn, a_w_in, a_b_in, a_w_dw, a_b_dw, a_ln_g, a_ln_b, a_w_out, a_b_out,
           b_w_in, b_w_dw, b_w_out, c_w_in, c_q_norm, c_k_norm, c_w_cmp, c_pe_cmp, c_w_out,
           ffn_w_gate, ffn_w_up, ffn_w_down):
    b, t, d = x_prompt.shape
    nb = x_sample.shape[0]
    depth = norm_mix.shape[0]
    xp = x_prompt
    xs = x_sample.reshape(nb, d)
    conv_a_p, conv_a_s, conv_b_p, conv_b_s = [], [], [], []
    kv_p, kv_s, win_p, win_s = [], [], [], []
    for i in range(depth):
        kind, j = i % 3, i // 3
        g = norm_mix[i]
        if kind == 0:
            wa = (g[None, :], a_w_in[j].astype(BF16), a_b_in[j][None, :], a_w_dw[j], a_b_dw[j][None, :],
                  a_ln_g[j][None, :], a_ln_b[j][None, :], a_w_out[j].astype(BF16), a_b_out[j][None, :])
            xp, st_p = _mixa_prompt(xp, *wa, 256)
            xs, st_s = _mixa_sample(xs, state_conv_a[j], *wa)
            conv_a_p.append(st_p)
            conv_a_s.append(st_s)
        elif kind == 1:
            wb = (g[None, :], b_w_in[j].astype(BF16), b_w_dw[j], b_w_out[j].astype(BF16))
            xp, st_p = _mixb_prompt(xp, *wb, 256)
            xs, st_s = _mixb_sample(xs, state_conv_b[j], *wb)
            conv_b_p.append(st_p)
            conv_b_s.append(st_s)
        else:
            xp, xs, rows_p, rows_s, wst_p, wst_s = _nsa_layer(
                xp, xs, g, cache_kv[j], state_kv_win[j], page_table, c_w_in[j], c_q_norm[j],
                c_k_norm[j], c_w_cmp[j], c_pe_cmp[j], c_w_out[j])
            kv_p.append(rows_p)
            kv_s.append(rows_s)
            win_p.append(wst_p)
            win_s.append(wst_s)
        wf = (norm_ffn[i][None, :], ffn_w_gate[i].astype(BF16), ffn_w_up[i].astype(BF16),
              ffn_w_down[i].astype(BF16))
        xp = _ffn(xp.reshape(b * t, d), *wf, 512).reshape(b, t, d)
        xs = _ffn(xs, *wf, nb)
    return (xp, xs.reshape(nb, 1, d), jnp.stack(conv_a_p), jnp.stack(conv_a_s), jnp.stack(conv_b_p),
            jnp.stack(conv_b_s), jnp.stack(kv_p), jnp.stack(kv_s), jnp.stack(win_p), jnp.stack(win_s))
```

```python
import functools
import math

import jax
import jax.numpy as jnp
from jax import lax
from jax.experimental import pallas as pl
from jax.experimental.pallas import tpu as pltpu

F32 = jnp.float32
BF16 = jnp.bfloat16

EPS = 1e-6
HEAD_DIM = 64
N_KV = 4
GROUP = 4
N_HEADS = N_KV * GROUP
ROT_DIM = HEAD_DIM // 4
ROPE_THETA = 500000.0
CMP_BLOCK = 32
CMP_STRIDE = 16
SEL_BLOCK = 64
N_SELECT = 16
WINDOW = 512
FORCE_SCORE = 1e9
MASK_BIAS = -1e9
M_INIT = -1e30
LOG2E = 1.4426950408889634

LANES = 128
SUBLANES = 8
VMEM_LIMIT = 56 * 1024 * 1024


def _cparams(n_grid):
    return pltpu.CompilerParams(dimension_semantics=("arbitrary",) * n_grid,
                                vmem_limit_bytes=VMEM_LIMIT)


def _const_spec(shape):
    nd = len(shape)
    return pl.BlockSpec(shape, lambda *_: (0,) * nd, pipeline_mode=pl.Buffered(1))


def _dot(a, b):
    return jnp.dot(a, b, preferred_element_type=F32)


def _dot_t(a, b):
    return lax.dot_general(a, b, (((1,), (1,)), ((), ())), preferred_element_type=F32)


def _rms(x, g):
    return x * lax.rsqrt(jnp.mean(x * x, axis=-1, keepdims=True) + EPS) * g


def _split3(x):
    hi = x.astype(BF16)
    r = x - hi.astype(F32)
    mid = r.astype(BF16)
    lo = (r - mid.astype(F32)).astype(BF16)
    return hi, mid, lo


def _ffn_body(x_ref, g_ref, wg_ref, wu_ref, wd_ref, o_ref, *, fc):
    x = x_ref[...]
    xn = _rms(x, g_ref[...]).astype(BF16)
    d_ff = wg_ref.shape[1]
    acc = None
    for c in range(d_ff // fc):
        hg = _dot(xn, wg_ref[:, c * fc:(c + 1) * fc])
        hu = _dot(xn, wu_ref[:, c * fc:(c + 1) * fc])
        a = (hg * jax.nn.sigmoid(hg) * hu).astype(BF16)
        d = _dot(a, wd_ref[c * fc:(c + 1) * fc, :])
        acc = d if acc is None else acc + d
    o_ref[...] = x + acc


def _ffn(x, g, wg, wu, wd, tm):
    m, d = x.shape
    d_ff = wg.shape[1]
    return pl.pallas_call(
        functools.partial(_ffn_body, fc=256),
        grid=(m // tm,),
        in_specs=[pl.BlockSpec((tm, d), lambda i: (i, 0)),
                  _const_spec((1, d)), _const_spec((d, d_ff)), _const_spec((d, d_ff)),
                  _const_spec((d_ff, d))],
        out_specs=pl.BlockSpec((tm, d), lambda i: (i, 0)),
        out_shape=jax.ShapeDtypeStruct((m, d), F32),
        compiler_params=_cparams(1), name="ffn",
    )(x, g, wg, wu, wd)


def _linres_body(x_ref, o_ref, w_ref, y_ref):
    y_ref[...] = x_ref[...] + _dot(o_ref[...], w_ref[...])


def _linres(x, o, w, tm):
    m, d = x.shape
    k = o.shape[1]
    return pl.pallas_call(
        _linres_body, grid=(m // tm,),
        in_specs=[pl.BlockSpec((tm, d), lambda i: (i, 0)),
                  pl.BlockSpec((tm, k), lambda i: (i, 0)), _const_spec((k, d))],
        out_specs=pl.BlockSpec((tm, d), lambda i: (i, 0)),
        out_shape=jax.ShapeDtypeStruct((m, d), F32),
        compiler_params=_cparams(1), name="linres",
    )(x, o, w)


A_HALO = 32
A_CHUNK = 32


def _ln_swish(c, g, b):
    mu = jnp.mean(c, axis=-1, keepdims=True)
    var = jnp.mean(jnp.square(c - mu), axis=-1, keepdims=True)
    y = (c - mu) * lax.rsqrt(var + EPS) * g + b
    return y * jax.nn.sigmoid(y)


def _mixa_prompt_body(x_ref, g_ref, win_ref, bin_ref, wdw8_ref, bdw_ref, lng_ref, lnb_ref,
                      wout_ref, bout_ref, y_ref, tail_ref, ubuf, ushift, cbuf, *, width):
    i = pl.program_id(1)
    tm, d = x_ref.shape[1], x_ref.shape[2]

    @pl.when(i == 0)
    def _():
        ubuf[0:A_HALO, :] = jnp.zeros((A_HALO, d), F32)

    x = x_ref[0]
    xn = _rms(x, g_ref[...]).astype(BF16)
    h = _dot(xn, win_ref[...]) + bin_ref[...]
    u = h[:, :d] * jax.nn.sigmoid(h[:, d:])
    ubuf[A_HALO:, :] = u
    span = tm + A_HALO - SUBLANES
    for s in range(1, SUBLANES):
        ushift[s - 1] = ubuf[s:s + span, :]
    off = A_HALO - (width - 1)
    for r0 in range(0, tm, A_CHUNK):
        acc = None
        for k in range(width):
            s = (off + k) % SUBLANES
            base = r0 + off + k - s
            rows = ubuf[base:base + A_CHUNK, :] if s == 0 else ushift[s - 1, base:base + A_CHUNK, :]
            t = jnp.concatenate([wdw8_ref[k]] * (A_CHUNK // SUBLANES), axis=0) * rows
            acc = t if acc is None else acc + t
        c = _ln_swish(acc + bdw_ref[...], lng_ref[...], lnb_ref[...])
        cbuf[r0:r0 + A_CHUNK, :] = c.astype(BF16)
    y_ref[0] = x + _dot(cbuf[...], wout_ref[...]) + bout_ref[...]
    last = ubuf[tm:tm + A_HALO, :]
    ubuf[0:A_HALO, :] = last

    @pl.when(i == pl.num_programs(1) - 1)
    def _():
        tail_ref[0] = last


def _mixa_prompt(x, g, w_in, b_in, w_dw, b_dw, ln_g, ln_b, w_out, b_out, tm):
    b, t, d = x.shape
    width = w_dw.shape[0]
    y, tail = pl.pallas_call(
        functools.partial(_mixa_prompt_body, width=width),
        grid=(b, t // tm),
        in_specs=[pl.BlockSpec((1, tm, d), lambda bi, i: (bi, i, 0)),
                  _const_spec((1, d)), _const_spec((d, 2 * d)), _const_spec((1, 2 * d)),
                  _const_spec((width, SUBLANES, d)), _const_spec((1, d)), _const_spec((1, d)),
                  _const_spec((1, d)), _const_spec((d, d)), _const_spec((1, d))],
        out_specs=[pl.BlockSpec((1, tm, d), lambda bi, i: (bi, i, 0)),
                   pl.BlockSpec((1, A_HALO, d), lambda bi, i: (bi, 0, 0))],
        out_shape=[jax.ShapeDtypeStruct((b, t, d), F32),
                   jax.ShapeDtypeStruct((b, A_HALO, d), F32)],
        scratch_shapes=[pltpu.VMEM((tm + A_HALO, d), F32),
                        pltpu.VMEM((SUBLANES - 1, tm + A_HALO - SUBLANES, d), F32),
                        pltpu.VMEM((tm, d), BF16)],
        compiler_params=_cparams(2), name="mixa_prompt",
    )(x, g, w_in, b_in, jnp.broadcast_to(w_dw[:, None, :], (width, SUBLANES, d)), b_dw, ln_g, ln_b,
      w_out, b_out)
    return y, tail[:, A_HALO - (width - 1):]


def _mixa_sample_body(x_ref, hist_ref, g_ref, win_ref, bin_ref, wdw_ref, bdw_ref, lng_ref,
                      lnb_ref, wout_ref, bout_ref, y_ref, st_ref, *, width):
    d = x_ref.shape[1]
    x = x_ref[...]
    xn = _rms(x, g_ref[...]).astype(BF16)
    h = _dot(xn, win_ref[...]) + bin_ref[...]
    u = h[:, :d] * jax.nn.sigmoid(h[:, d:])
    acc = None
    for k in range(width - 1):
        row = hist_ref[:, k, :]
        t = wdw_ref[k:k + 1, :] * row
        acc = t if acc is None else acc + t
        if k >= 1:
            st_ref[:, k - 1, :] = row
    acc = acc + wdw_ref[width - 1:width, :] * u
    st_ref[:, width - 2, :] = u
    c = _ln_swish(acc + bdw_ref[...], lng_ref[...], lnb_ref[...])
    y_ref[...] = x + _dot(c.astype(BF16), wout_ref[...]) + bout_ref[...]


def _mixa_sample(x, hist, g, w_in, b_in, w_dw, b_dw, ln_g, ln_b, w_out, b_out):
    n, d = x.shape
    width = w_dw.shape[0]
    return pl.pallas_call(
        functools.partial(_mixa_sample_body, width=width),
        out_shape=[jax.ShapeDtypeStruct((n, d), F32),
                   jax.ShapeDtypeStruct((n, width - 1, d), F32)],
        compiler_params=pltpu.CompilerParams(vmem_limit_bytes=VMEM_LIMIT), name="mixa_sample",
    )(x, hist, g, w_in, b_in, w_dw, b_dw, ln_g, ln_b, w_out, b_out)


B_HALO = 8


def _mixb_prompt_body(x_ref, g_ref, win_ref, wdw_ref, wout_ref, y_ref, tail_ref, cbuf, *, width):
    i = pl.program_id(1)
    tm, d = x_ref.shape[1], x_ref.shape[2]

    @pl.when(i == 0)
    def _():
        cbuf[0:B_HALO, :] = jnp.zeros((B_HALO, d), F32)

    x = x_ref[0]
    xn = _rms(x, g_ref[...]).astype(BF16)
    h = _dot(xn, win_ref[...])
    cbuf[B_HALO:, :] = h[:, d:2 * d] * h[:, 2 * d:]
    off = B_HALO - (width - 1)
    conv = None
    for k in range(width):
        t = wdw_ref[k:k + 1, :] * cbuf[off + k:off + k + tm, :]
        conv = t if conv is None else conv + t
    z = (h[:, :d] * conv).astype(BF16)
    y_ref[0] = x + _dot(z, wout_ref[...])
    last = cbuf[tm:tm + B_HALO, :]
    cbuf[0:B_HALO, :] = last

    @pl.when(i == pl.num_programs(1) - 1)
    def _():
        tail_ref[0] = last


def _mixb_prompt(x, g, w_in, w_dw, w_out, tm):
    b, t, d = x.shape
    width = w_dw.shape[0]
    y, tail = pl.pallas_call(
        functools.partial(_mixb_prompt_body, width=width),
        grid=(b, t // tm),
        in_specs=[pl.BlockSpec((1, tm, d), lambda bi, i: (bi, i, 0)),
                  _const_spec((1, d)), _const_spec((d, 3 * d)), _const_spec((width, d)),
                  _const_spec((d, d))],
        out_specs=[pl.BlockSpec((1, tm, d), lambda bi, i: (bi, i, 0)),
                   pl.BlockSpec((1, B_HALO, d), lambda bi, i: (bi, 0, 0))],
        out_shape=[jax.ShapeDtypeStruct((b, t, d), F32),
                   jax.ShapeDtypeStruct((b, B_HALO, d), F32)],
        scratch_shapes=[pltpu.VMEM((tm + B_HALO, d), F32)],
        compiler_params=_cparams(2), name="mixb_prompt",
    )(x, g, w_in, w_dw, w_out)
    return y, tail[:, B_HALO - (width - 1):]


def _mixb_sample_body(x_ref, hist_ref, g_ref, win_ref, wdw_ref, wout_ref, y_ref, st_ref, *, width):
    d = x_ref.shape[1]
    x = x_ref[...]
    xn = _rms(x, g_ref[...]).astype(BF16)
    h = _dot(xn, win_ref[...])
    cv = h[:, d:2 * d] * h[:, 2 * d:]
    conv = None
    for k in range(width - 1):
        row = hist_ref[:, k, :]
        t = wdw_ref[k:k + 1, :] * row
        conv = t if conv is None else conv + t
        if k >= 1:
            st_ref[:, k - 1, :] = row
    conv = conv + wdw_ref[width - 1:width, :] * cv
    st_ref[:, width - 2, :] = cv
    z = (h[:, :d] * conv).astype(BF16)
    y_ref[...] = x + _dot(z, wout_ref[...])


def _mixb_sample(x, hist, g, w_in, w_dw, w_out):
    n, d = x.shape
    width = w_dw.shape[0]
    return pl.pallas_call(
        functools.partial(_mixb_sample_body, width=width),
        out_shape=[jax.ShapeDtypeStruct((n, d), F32),
                   jax.ShapeDtypeStruct((n, width - 1, d), F32)],
        compiler_params=pltpu.CompilerParams(vmem_limit_bytes=VMEM_LIMIT), name="mixb_sample",
    )(x, hist, g, w_in, w_dw, w_out)


def _rope_tables(pos):
    half = ROT_DIM // 2
    inv = jnp.exp(jnp.arange(half, dtype=F32) * (-2.0 * math.log(ROPE_THETA) / ROT_DIM))
    ang = pos.astype(F32)[:, None] * inv[None, :]
    cos, sin = jnp.cos(ang), jnp.sin(ang)
    r = jnp.arange(LANES) % HEAD_DIM
    c = jnp.where(r[None, :] < ROT_DIM, cos[:, r % half], 1.0)
    s1 = jnp.where(r[None, :] < half, -sin[:, r % half], 0.0)
    s2 = jnp.where((r[None, :] >= half) & (r[None, :] < ROT_DIM), sin[:, r % half], 0.0)
    return c, s1, s2


def _head_norm_rope(blk, gvec, c, s1, s2):
    lane = lax.broadcasted_iota(jnp.int32, (1, LANES), 1)
    first = lane < HEAD_DIM
    x2 = blk * blk
    ms_a = jnp.sum(jnp.where(first, x2, 0.0), axis=-1, keepdims=True)
    ms_b = jnp.sum(jnp.where(first, 0.0, x2), axis=-1, keepdims=True)
    ms = jnp.where(first, ms_a, ms_b) * (1.0 / HEAD_DIM)
    y = blk * lax.rsqrt(ms + EPS) * gvec
    return y * c + pltpu.roll(y, LANES - ROT_DIM // 2, 1) * s1 + pltpu.roll(y, ROT_DIM // 2, 1) * s2


GATE_ROWS = 16


def _nsa_proj_body(x_ref, g_ref, wqkv_ref, wg_ref, qn_ref, kn_ref, c_ref, s1_ref, s2_ref, *outs,
                   prompt_layout):
    if prompt_layout:
        qt_ref, paged_ref, win_ref, cmp_ref, kaug_ref, vst_ref, kw_ref, vwt_ref, gt_ref = outs
    else:
        q_ref, paged_ref, win_ref, gate_ref = outs
    x = x_ref[0]
    tm = x.shape[0]
    xn = _rms(x, g_ref[...]).astype(BF16)
    h = _dot(xn, wqkv_ref[...])
    gh = _dot(xn, wg_ref[...])
    c, s1, s2 = c_ref[...], s1_ref[...], s2_ref[...]
    nq = N_HEADS * HEAD_DIM
    kvw = N_KV * HEAD_DIM
    scale = HEAD_DIM ** -0.5 * LOG2E

    def put_heads_t(dst, pair, blk_t):
        for j in range(tm // LANES):
            dst[0, 2 * pair, j] = blk_t[:HEAD_DIM, j * LANES:(j + 1) * LANES]
            dst[0, 2 * pair + 1, j] = blk_t[HEAD_DIM:, j * LANES:(j + 1) * LANES]

    for cb in range(nq // LANES):
        blk = _head_norm_rope(h[:, cb * LANES:(cb + 1) * LANES], qn_ref[...], c, s1, s2) * scale
        if prompt_layout:
            bt = blk.T.astype(BF16)
            qt_ref[0, 2 * cb] = bt[:HEAD_DIM]
            qt_ref[0, 2 * cb + 1] = bt[HEAD_DIM:]
        else:
            q_ref[0, :, cb * LANES:(cb + 1) * LANES] = blk.astype(BF16)
    if prompt_layout:
        t_abs = pl.program_id(1) * tm + lax.broadcasted_iota(jnp.int32, (tm, 1), 0)
        lane = lax.broadcasted_iota(jnp.int32, (1, LANES), 1)
        onehot = jnp.where(t_abs // SEL_BLOCK == lane, 1.0, 0.0).astype(BF16)
    for br in range(3):
        kbase = nq + br * 2 * kvw
        for cb in range(kvw // LANES):
            kblk = _head_norm_rope(h[:, kbase + cb * LANES:kbase + (cb + 1) * LANES],
                                   kn_ref[br:br + 1, :], c, s1, s2)
            vblk = h[:, kbase + kvw + cb * LANES:kbase + kvw + (cb + 1) * LANES]
            if br < 2:
                paged_ref[0, :, br * 2 * kvw + cb * LANES:br * 2 * kvw + (cb + 1) * LANES] = kblk
                paged_ref[0, :, br * 2 * kvw + kvw + cb * LANES:
                          br * 2 * kvw + kvw + (cb + 1) * LANES] = vblk
            else:
                win_ref[0, :, cb * LANES:(cb + 1) * LANES] = kblk
                win_ref[0, :, kvw + cb * LANES:kvw + (cb + 1) * LANES] = vblk
            if not prompt_layout:
                continue
            if br == 0:
                cmp_ref[0, cb] = kblk
                cmp_ref[0, kvw // LANES + cb] = vblk
            elif br == 1:
                kaug_ref[0, 2 * cb] = jnp.concatenate([onehot, kblk.astype(BF16)], axis=1)
                kaug_ref[0, 2 * cb + 1] = jnp.concatenate(
                    [onehot, pltpu.roll(kblk, HEAD_DIM, 1).astype(BF16)], axis=1)
                put_heads_t(vst_ref, cb, vblk.T.astype(BF16))
            else:
                kb = kblk.astype(BF16)
                kw_ref[0, 2 * cb] = kb[:, :HEAD_DIM]
                kw_ref[0, 2 * cb + 1] = kb[:, HEAD_DIM:]
                put_heads_t(vwt_ref, cb, vblk.T.astype(BF16))
    if prompt_layout:
        for hh in range(N_KV):
            gt_ref[0, hh] = gh[:, hh * LANES:(hh + 1) * LANES].T[:GATE_ROWS]
    else:
        gate_ref[0] = gh


def _nsa_proj(x, g, wqkv, wgate, qn, kn, rope, tm, prompt_layout):
    b, t, d = x.shape
    c, s1, s2 = rope
    nqkv = wqkv.shape[1]
    kvw = N_KV * HEAD_DIM
    row = lambda w, dt: (jax.ShapeDtypeStruct((b, t, w), dt),
                         pl.BlockSpec((1, tm, w), lambda bi, i: (bi, i, 0)))
    tab_spec = pl.BlockSpec((tm, LANES), lambda bi, i: (i, 0))
    if prompt_layout:
        vt = (jax.ShapeDtypeStruct((b, N_KV, t // LANES, HEAD_DIM, LANES), BF16),
              pl.BlockSpec((1, N_KV, tm // LANES, HEAD_DIM, LANES), lambda bi, i: (bi, 0, i, 0, 0)))
        outs = [(jax.ShapeDtypeStruct((b, N_HEADS, HEAD_DIM, t), BF16),
                 pl.BlockSpec((1, N_HEADS, HEAD_DIM, tm), lambda bi, i: (bi, 0, 0, i))),
                row(4 * kvw, F32), row(2 * kvw, F32),
                (jax.ShapeDtypeStruct((b, 4, t, LANES), F32),
                 pl.BlockSpec((1, 4, tm, LANES), lambda bi, i: (bi, 0, i, 0))),
                (jax.ShapeDtypeStruct((b, N_KV, t, 2 * LANES), BF16),
                 pl.BlockSpec((1, N_KV, tm, 2 * LANES), lambda bi, i: (bi, 0, i, 0))),
                vt,
                (jax.ShapeDtypeStruct((b, N_KV, t, HEAD_DIM), BF16),
                 pl.BlockSpec((1, N_KV, tm, HEAD_DIM), lambda bi, i: (bi, 0, i, 0))),
                vt,
                (jax.ShapeDtypeStruct((b, N_KV, GATE_ROWS, t), F32),
                 pl.BlockSpec((1, N_KV, GATE_ROWS, tm), lambda bi, i: (bi, 0, 0, i)))]
    else:
        outs = [row(N_HEADS * HEAD_DIM, BF16), row(4 * kvw, F32), row(2 * kvw, F32),
                row(N_KV * LANES, F32)]
    return pl.pallas_call(
        functools.partial(_nsa_proj_body, prompt_layout=prompt_layout), grid=(b, t // tm),
        in_specs=[pl.BlockSpec((1, tm, d), lambda bi, i: (bi, i, 0)), _const_spec((1, d)),
                  _const_spec((d, nqkv)), _const_spec((d, N_KV * LANES)), _const_spec((1, LANES)),
                  _const_spec((3, LANES)), tab_spec, tab_spec, tab_spec],
        out_specs=[o[1] for o in outs], out_shape=[o[0] for o in outs],
        compiler_params=_cparams(2), name="nsa_proj" if prompt_layout else "nsa_proj_sample",
    )(x, g, wqkv, wgate, qn, kn, c, s1, s2)


def _cmp_const(w_ref, pe_ref, kv):
    cst = jnp.sum(w_ref[kv] * pe_ref[kv], axis=0, keepdims=True)
    return jnp.concatenate([cst] * N_KV, axis=1)


SHIFT_PAD = 8


def _compress_rows(load, bd_ref, kv, n_blk, shift_buf):
    first = second = None
    for l in range(CMP_STRIDE):
        xl = jnp.concatenate([load(0, l), load(1, l)], axis=1).astype(BF16)
        ta = _dot(xl, bd_ref[kv, l])
        tb = _dot(xl, bd_ref[kv, CMP_STRIDE + l])
        first = ta if first is None else first + ta
        second = tb if second is None else second + tb
    shift_buf[0:n_blk, :] = second
    shift_buf[n_blk:n_blk + SHIFT_PAD, :] = jnp.zeros((SHIFT_PAD, shift_buf.shape[1]), F32)
    return first + shift_buf[1:n_blk + 1, :]


def _compress_body(x_ref, bd_ref, w_ref, pe_ref, kc_ref, vct_ref, shift_buf):
    kv = pl.program_id(1)
    n_blk = x_ref.shape[2] // CMP_STRIDE
    out = _compress_rows(lambda c, l: x_ref[0, c, pl.ds(l, n_blk, stride=CMP_STRIDE), :],
                         bd_ref, kv, n_blk, shift_buf) + _cmp_const(w_ref, pe_ref, kv)

    @pl.when(kv == 0)
    def _():
        for hh in range(N_KV):
            kc_ref[0, hh] = out[:, hh * HEAD_DIM:(hh + 1) * HEAD_DIM].astype(BF16)

    @pl.when(kv == 1)
    def _():
        out_t = out.T.astype(BF16)
        for hh in range(N_KV):
            vct_ref[0, hh] = out_t[hh * HEAD_DIM:(hh + 1) * HEAD_DIM]


def _compress(cmp_rows, bd, w_cmp, pe_col):
    b, _, t, _ = cmp_rows.shape
    n_blk = t // CMP_STRIDE
    return pl.pallas_call(
        _compress_body, grid=(b, 2),
        in_specs=[pl.BlockSpec((1, 2, t, LANES), lambda bi, kv: (bi, kv, 0, 0)),
                  _const_spec(bd.shape), _const_spec(w_cmp.shape), _const_spec(pe_col.shape)],
        out_specs=[pl.BlockSpec((1, N_KV, n_blk, HEAD_DIM), lambda bi, kv: (bi, 0, 0, 0)),
                   pl.BlockSpec((1, N_KV, HEAD_DIM, n_blk), lambda bi, kv: (bi, 0, 0, 0))],
        out_shape=[jax.ShapeDtypeStruct((b, N_KV, n_blk, HEAD_DIM), BF16),
                   jax.ShapeDtypeStruct((b, N_KV, HEAD_DIM, n_blk), BF16)],
        scratch_shapes=[pltpu.VMEM((n_blk + SHIFT_PAD, N_KV * HEAD_DIM), F32)],
        compiler_params=_cparams(2), name="nsa_compress",
    )(cmp_rows, bd, w_cmp, pe_col)


def _softmax(s, mask, axis):
    s = jnp.where(mask, s, -jnp.inf)
    m = jnp.max(s, axis=axis, keepdims=True)
    m = jnp.where(m == -jnp.inf, 0.0, m)
    e = jnp.exp2(s - m)
    return e / jnp.maximum(jnp.sum(e, axis=axis, keepdims=True), 1e-30)


def _importance(p_sum, ov_ref, axis):
    hi, mid, lo = _split3(p_sum)
    ov = ov_ref[...]
    if axis == 0:
        return _dot(ov, hi) + _dot(ov, mid) + _dot(ov, lo)
    return _dot(hi, ov) + _dot(mid, ov) + _dot(lo, ov)


def _select_blocks(imp, q_pos, n_sel, axis):
    w = imp.shape[axis]
    j = lax.broadcasted_iota(jnp.int32, imp.shape, axis)
    cur = q_pos // SEL_BLOCK
    forced = (j == 0) | (j == cur) | (j == cur - 1)
    v = jnp.where(forced, FORCE_SCORE, imp)
    v = jnp.where((j * SEL_BLOCK > q_pos) | (j >= n_sel), -jnp.inf, v)
    sel = jnp.zeros(imp.shape, jnp.bool_)
    for _ in range(min(N_SELECT, n_sel)):
        mx = jnp.max(v, axis=axis, keepdims=True)
        idx = jnp.min(jnp.where(v == mx, j, w), axis=axis, keepdims=True)
        hit = (j == idx) & (mx > -jnp.inf)
        sel = sel | hit
        v = jnp.where(hit, -jnp.inf, v)
    return sel


def _nsa_attn_body(qt_ref, kc_ref, vct_ref, kaug_ref, vst_ref, kw_ref, vwt_ref, gt_ref, ovt_ref,
                   wmask_ref, dmask_ref, o_ref, s_a, s_b, *, tq, tk):
    t0 = pl.program_id(2) * tq
    cols = GROUP * tq
    n_cmp = kc_ref.shape[2]
    t_len = kaug_ref.shape[2]
    qt = jnp.concatenate([qt_ref[0, g] for g in range(GROUP)], axis=1)
    col_t = t0 + lax.rem(lax.broadcasted_iota(jnp.int32, (1, cols), 1), tq)

    ci = lax.broadcasted_iota(jnp.int32, (n_cmp, 1), 0)
    p_c = _softmax(_dot(kc_ref[0, 0], qt), ci * CMP_STRIDE + (CMP_BLOCK - 1) <= col_t, 0)
    o_c = _dot(vct_ref[0, 0], p_c.astype(BF16))
    p_sum = p_c[:, 0:tq]
    for g in range(1, GROUP):
        p_sum = p_sum + p_c[:, g * tq:(g + 1) * tq]

    q_pos = t0 + lax.broadcasted_iota(jnp.int32, (1, tq), 1)
    sel = _select_blocks(_importance(p_sum, ovt_ref, 0), q_pos, t_len // SEL_BLOCK, 0)
    bias = jnp.where(sel, 0.0, MASK_BIAS).astype(BF16)
    rhs = jnp.concatenate([jnp.concatenate([bias] * GROUP, axis=1), qt,
                           jnp.zeros((LANES - HEAD_DIM, cols), BF16)], axis=0)

    vt_blocks = tk // LANES

    def scores(kt, buf):
        k0 = pl.multiple_of(kt * tk, tk)
        buf[...] = _dot(kaug_ref[0, 0, pl.ds(k0, tk), :], rhs)

    def consume(buf, kt, carry, causal):
        m, l, acc = carry
        s = buf[...]
        if causal:
            tri = dmask_ref[(t0 - kt * tk) // tq]
            s = s + jnp.concatenate([tri] * GROUP, axis=1)
        m_new = jnp.maximum(m, jnp.max(s, axis=0, keepdims=True))
        alpha = jnp.exp2(m - m_new)
        p = jnp.exp2(s - m_new)
        l = alpha * l + jnp.sum(p, axis=0, keepdims=True)
        vt = jnp.concatenate([vst_ref[0, 0, kt * vt_blocks + j] for j in range(vt_blocks)], axis=1)
        acc = alpha * acc + _dot(vt, p.astype(BF16))
        return m_new, l, acc

    def pair(j, carry):
        scores(2 * j + 1, s_b)
        carry = consume(s_a, 2 * j, carry, False)
        scores(2 * j + 2, s_a)
        return consume(s_b, 2 * j + 1, carry, False)

    def tail_even(carry):
        return consume(s_a, kt_diag, carry, True)

    def tail_odd(carry):
        scores(kt_diag, s_b)
        carry = consume(s_a, kt_diag - 1, carry, False)
        return consume(s_b, kt_diag, carry, True)

    kt_diag = t0 // tk
    carry = (jnp.full((1, cols), M_INIT, F32), jnp.zeros((1, cols), F32),
             jnp.zeros((HEAD_DIM, cols), F32))
    scores(0, s_a)
    carry = lax.fori_loop(0, kt_diag // 2, pair, carry)
    _, l_s, acc_s = lax.cond(kt_diag % 2 == 0, tail_even, tail_odd, carry)
    o_s = acc_s / l_s

    nw = WINDOW + tq
    w0 = pl.multiple_of(jnp.maximum(t0 - WINDOW, 0), tq)
    band = wmask_ref[jnp.minimum(pl.program_id(2), wmask_ref.shape[0] - 1)]
    s_w = _dot(kw_ref[0, 0, pl.ds(w0, nw), :], qt) + jnp.concatenate([band] * GROUP, axis=1)
    e_w = jnp.exp2(s_w - jnp.max(s_w, axis=0, keepdims=True))
    vwt = jnp.concatenate([vwt_ref[0, 0, w0 // LANES + j] for j in range(nw // LANES)], axis=1)
    o_w = _dot(vwt, e_w.astype(BF16)) / jnp.sum(e_w, axis=0, keepdims=True)

    gate = jax.nn.sigmoid(gt_ref[0, 0])
    outs = []
    for g in range(GROUP):
        sl = slice(g * tq, (g + 1) * tq)
        outs.append(gate[3 * g:3 * g + 1] * o_c[:, sl] + gate[3 * g + 1:3 * g + 2] * o_s[:, sl]
                    + gate[3 * g + 2:3 * g + 3] * o_w[:, sl])
    o_ref[0] = jnp.concatenate(outs, axis=0).T.astype(BF16)


def _nsa_attn(qt, kc, vct, kaug, vst, kw, vwt, gt, overlap_t, tq, tk):
    b, _, _, t = qt.shape
    n_cmp = kc.shape[2]
    assert t // SEL_BLOCK <= LANES and tq % LANES == 0 and tk % tq == 0 and t % tk == 0
    assert WINDOW % tq == 0 and t >= WINDOW + tq
    kr = jnp.arange(WINDOW + tq)[None, :, None]
    tr = jnp.arange(tq)[None, None, :]
    rel = kr - tr - jnp.minimum(jnp.arange(WINDOW // tq + 1) * tq, WINDOW)[:, None, None]
    wmask = jnp.where((rel <= 0) & (rel >= -WINDOW), 0.0, MASK_BIAS).astype(F32)
    rel = jnp.arange(tk)[None, :, None] - tr - (jnp.arange(tk // tq) * tq)[:, None, None]
    dmask = jnp.where(rel <= 0, 0.0, MASK_BIAS).astype(F32)
    per_head = lambda *shape: pl.BlockSpec((1, 1) + shape, lambda bi, h, i: (bi, h) + (0,) * len(shape))
    return pl.pallas_call(
        functools.partial(_nsa_attn_body, tq=tq, tk=tk),
        grid=(b, N_KV, t // tq),
        in_specs=[pl.BlockSpec((1, GROUP, HEAD_DIM, tq), lambda bi, h, i: (bi, h, 0, i)),
                  per_head(n_cmp, HEAD_DIM), per_head(HEAD_DIM, n_cmp), per_head(t, 2 * LANES),
                  per_head(t // LANES, HEAD_DIM, LANES), per_head(t, HEAD_DIM),
                  per_head(t // LANES, HEAD_DIM, LANES),
                  pl.BlockSpec((1, 1, GATE_ROWS, tq), lambda bi, h, i: (bi, h, 0, i)),
                  _const_spec(overlap_t.shape), _const_spec(wmask.shape), _const_spec(dmask.shape)],
        out_specs=pl.BlockSpec((1, tq, GROUP * HEAD_DIM), lambda bi, h, i: (bi, i, h)),
        out_shape=jax.ShapeDtypeStruct((b, t, N_HEADS * HEAD_DIM), BF16),
        scratch_shapes=[pltpu.VMEM((tk, GROUP * tq), F32), pltpu.VMEM((tk, GROUP * tq), F32)],
        compiler_params=_cparams(3), name="nsa_attn",
    )(qt, kc, vct, kaug, vst, kw, vwt, gt, overlap_t, wmask, dmask)


S_ROWS = 32
S_HPAD = 8


def _nsa_sample_body(pt_ref, *refs, n_pg, past_len):
    pages = refs[:n_pg]
    (win_ref, q_ref, new_ref, wnew_ref, g_ref, bd_ref, w_ref, pe_ref, ov_ref, e_ref,
     o_ref, wout_ref, kcbuf, vcbuf, kst_buf, vst_buf, shift_buf) = refs[n_pg:]
    step = pl.program_id(1)
    kvw = N_KV * HEAD_DIM
    psz = pages[0].shape[2]
    n_pages = kst_buf.shape[0]
    for i in range(n_pg):
        pg = step * n_pg + i
        r0 = pl.multiple_of(pg * psz, psz)
        for buf, f0 in ((kcbuf, 0), (vcbuf, kvw)):
            rows = pages[i][0, f0:f0 + kvw, :].T
            for c in range(2):
                buf[c, pl.ds(r0, psz), :] = rows[:, c * LANES:(c + 1) * LANES]
        kst_buf[pg] = pages[i][0, 2 * kvw:3 * kvw, :].astype(BF16)
        vst_buf[pg] = pages[i][0, 3 * kvw:4 * kvw, :].astype(BF16)

    @pl.when(step == pl.num_programs(1) - 1)
    def _():
        n_blk = past_len // CMP_STRIDE
        new = new_ref[0]
        cmp_of = lambda buf, kv: (_compress_rows(
            lambda c, l: buf[c, pl.ds(l, n_blk, stride=CMP_STRIDE), :], bd_ref, kv, n_blk, shift_buf)
            + _cmp_const(w_ref, pe_ref, kv)).astype(BF16)
        kc = cmp_of(kcbuf, 0)
        vc = cmp_of(vcbuf, 1)

        row = lax.broadcasted_iota(jnp.int32, (S_ROWS, kvw), 0)
        lane = lax.broadcasted_iota(jnp.int32, (S_ROWS, kvw), 1)
        own = lane // HEAD_DIM == lax.rem(row, S_HPAD)
        q = jnp.where(own, jnp.concatenate([q_ref[0]] * N_KV, axis=1), 0.0).astype(BF16)
        qf = q.astype(F32)

        def own_head(full):
            z = jnp.where(own, full, 0.0)
            out = z[:, 0:HEAD_DIM]
            for hh in range(1, N_KV):
                out = out + z[:, hh * HEAD_DIM:(hh + 1) * HEAD_DIM]
            return out

        def score_new(k_row):
            return jnp.sum(qf * k_row.astype(BF16).astype(F32), axis=-1, keepdims=True)

        def attend(s, mask, pv, s_new, v_row):
            if mask is not None:
                s = jnp.where(mask, s, -jnp.inf)
            m = jnp.maximum(jnp.max(s, axis=-1, keepdims=True), s_new)
            e = jnp.exp2(s - m)
            e_new = jnp.exp2(s_new - m)
            den = jnp.sum(e, axis=-1, keepdims=True) + e_new
            p = e / den
            p_new = (e_new / den).astype(BF16).astype(F32)
            return own_head(pv(p.astype(BF16)) + p_new * v_row.astype(BF16).astype(F32))

        pg_chunk = 8
        chunk_t = lambda buf, c: jnp.concatenate(
            [buf[c * pg_chunk + j] for j in range(pg_chunk)], axis=1)
        n_chunk = n_pages // pg_chunk
        ck = pg_chunk * psz

        q_pos = past_len
        ci = lax.broadcasted_iota(jnp.int32, (1, n_blk), 1)
        p_c = _softmax(_dot_t(q, kc), ci * CMP_STRIDE + (CMP_BLOCK - 1) <= q_pos, 1)
        o_c = own_head(_dot(p_c.astype(BF16), vc))
        p_sum = p_c[0:S_HPAD]
        for g in range(1, GROUP):
            p_sum = p_sum + p_c[g * S_HPAD:(g + 1) * S_HPAD]

        n_sel = past_len // SEL_BLOCK + 1
        sel = _select_blocks(_importance(p_sum, ov_ref, 1), jnp.full((S_HPAD, 1), q_pos, jnp.int32),
                             n_sel, 1)
        bias = jnp.where(sel, 0.0, MASK_BIAS).astype(BF16)
        bias = jnp.concatenate([bias] * GROUP, axis=0)

        s_s = (jnp.concatenate([_dot(q, chunk_t(kst_buf, c)) for c in range(n_chunk)], axis=1)
               + _dot_t(bias[:, 0:e_ref.shape[1]], e_ref[...]))

        def pv_sel(p):
            acc = None
            for c in range(n_chunk):
                t = _dot_t(p[:, c * ck:(c + 1) * ck], chunk_t(vst_buf, c))
                acc = t if acc is None else acc + t
            return acc

        o_s = attend(s_s, s_s > 0.5 * MASK_BIAS, pv_sel,
                     score_new(new[:, 2 * kvw:3 * kvw]), new[:, 3 * kvw:4 * kvw])

        wnew = wnew_ref[0]
        kw = win_ref[0, :, 0:kvw].astype(BF16)
        vw = win_ref[0, :, kvw:2 * kvw].astype(BF16)
        o_w = attend(_dot_t(q, kw), None, lambda p: _dot(p, vw),
                     score_new(wnew[:, 0:kvw]), wnew[:, kvw:2 * kvw])

        gate = jax.nn.sigmoid(g_ref[0])
        o_ref[0] = (gate[:, 0:1] * o_c + gate[:, 1:2] * o_s + gate[:, 2:3] * o_w).astype(BF16)

        wb = win_ref.shape[1]
        wout_ref[0, 0:wb - 1, :] = win_ref[0, 1:wb, :]
        wout_ref[0, wb - 1:wb, :] = wnew


def _nsa_sample(page_table, cache_t, win_buf, q_rows, new_rows, wnew_rows, gate_rows,
                bd, w_cmp, pe_col, overlap, onehot, n_pg):
    nb, n_pages = page_table.shape
    _, roww, psz = cache_t.shape
    past_len = n_pages * psz
    kvw = N_KV * HEAD_DIM
    wb = win_buf.shape[1]
    assert psz == LANES and n_pages % n_pg == 0 and n_pages % 8 == 0

    def page_spec(i):
        return pl.BlockSpec((1, roww, psz), lambda b, s, pt: (pt[b, s * n_pg + i], 0, 0))

    per_b = lambda shape: pl.BlockSpec((1,) + shape, lambda b, s, pt: (b, 0, 0))
    cst = lambda a: pl.BlockSpec(a.shape, lambda b, s, pt: (0,) * a.ndim, pipeline_mode=pl.Buffered(1))
    grid_spec = pltpu.PrefetchScalarGridSpec(
        num_scalar_prefetch=1, grid=(nb, n_pages // n_pg),
        in_specs=[page_spec(i) for i in range(n_pg)] + [
            per_b((wb, 2 * kvw)), per_b((S_ROWS, HEAD_DIM)), per_b((1, roww)), per_b((1, 2 * kvw)),
            per_b((S_ROWS, LANES)), cst(bd), cst(w_cmp), cst(pe_col), cst(overlap), cst(onehot)],
        out_specs=[per_b((S_ROWS, HEAD_DIM)), per_b((wb, 2 * kvw))],
        scratch_shapes=[pltpu.VMEM((2, past_len, LANES), F32), pltpu.VMEM((2, past_len, LANES), F32),
                        pltpu.VMEM((n_pages, kvw, psz), BF16), pltpu.VMEM((n_pages, kvw, psz), BF16),
                        pltpu.VMEM((past_len // CMP_STRIDE + SHIFT_PAD, kvw), F32)])
    return pl.pallas_call(
        functools.partial(_nsa_sample_body, n_pg=n_pg, past_len=past_len),
        grid_spec=grid_spec,
        out_shape=[jax.ShapeDtypeStruct((nb, S_ROWS, HEAD_DIM), BF16),
                   jax.ShapeDtypeStruct((nb, wb, 2 * kvw), F32)],
        compiler_params=_cparams(2), name="nsa_sample",
    )(page_table, *([cache_t] * n_pg), win_buf, q_rows, new_rows, wnew_rows, gate_rows,
      bd, w_cmp, pe_col, overlap, onehot)


def _overlap_matrix(n_cmp, width):
    ci = jnp.arange(n_cmp)[:, None] * CMP_STRIDE
    start = jnp.arange(width)[None, :] * SEL_BLOCK
    return ((ci < start + SEL_BLOCK) & (ci + CMP_BLOCK > start)).astype(BF16)


def _block_onehot(t, width):
    return (jnp.arange(t)[:, None] // SEL_BLOCK == jnp.arange(width)[None, :]).astype(BF16)


def _nsa_weights(w_in, q_norm, k_norm, w_cmp, pe_cmp):
    nq = N_HEADS * HEAD_DIM
    nkv = 6 * N_KV * HEAD_DIM
    d = w_in.shape[0]
    wqkv = w_in[:, :nq + nkv].astype(BF16)
    wg = w_in[:, nq + nkv:].reshape(d, N_KV, GROUP * 3)
    wg = jnp.pad(wg, ((0, 0), (0, 0), (0, LANES - GROUP * 3))).reshape(d, N_KV * LANES).astype(BF16)
    qn = jnp.tile(q_norm, LANES // HEAD_DIM)[None, :]
    kn = jnp.tile(k_norm, (1, LANES // HEAD_DIM))
    w4 = w_cmp.reshape(2, CMP_BLOCK, HEAD_DIM, HEAD_DIM)
    eye = jnp.eye(N_KV, dtype=w_cmp.dtype)
    bd = (eye[None, None, :, None, :, None] * w4[:, :, None, :, None, :]).reshape(
        2, CMP_BLOCK, N_KV * HEAD_DIM, N_KV * HEAD_DIM).astype(BF16)
    pe_col = pe_cmp.reshape(2, CMP_BLOCK * HEAD_DIM, 1)
    return wqkv, wg, qn, kn, bd, pe_col


def _nsa_layer(xp, xs, g, cache_l, win_buf, page_table, w_in, q_norm, k_norm, w_cmp, pe_cmp, w_out):
    b, t, d = xp.shape
    nb = xs.shape[0]
    wqkv, wg, qn, kn, bd, pe_col = _nsa_weights(w_in, q_norm, k_norm, w_cmp, pe_cmp)
    w_out_b = w_out.astype(BF16)
    g2 = g[None, :]

    rope_p = _rope_tables(jnp.arange(t, dtype=jnp.int32))
    qt, paged, win, cmp_rows, kaug, vst, kw, vwt, gt = _nsa_proj(
        xp, g2, wqkv, wg, qn, kn, rope_p, 256, True)
    kc, vct = _compress(cmp_rows, bd, w_cmp, pe_col)
    o = _nsa_attn(qt, kc, vct, kaug, vst, kw, vwt, gt,
                  _overlap_matrix(t // CMP_STRIDE, LANES).T, 512, 512)
    yp = _linres(xp.reshape(b * t, d), o.reshape(b * t, d), w_out_b, 512).reshape(b, t, d)
    win_p = win[:, t - min(WINDOW, t):]

    n_pages = page_table.shape[1]
    psz = cache_l.shape[1]
    past_len = n_pages * psz
    rope_s = _rope_tables(jnp.full((nb,), past_len, jnp.int32))
    q_s, paged_s, win_s, gates_s = _nsa_proj(xs[None], g2, wqkv, wg, qn, kn, rope_s, nb, False)
    qr = q_s.reshape(nb, N_KV, GROUP, HEAD_DIM).transpose(0, 2, 1, 3)
    qr = jnp.pad(qr, ((0, 0), (0, 0), (0, S_HPAD - N_KV), (0, 0))).reshape(nb, S_ROWS, HEAD_DIM)
    gr = gates_s.reshape(nb, N_KV, LANES)[:, :, :GROUP * 3].reshape(nb, N_KV, GROUP, 3)
    gr = gr.transpose(0, 2, 1, 3)
    gr = jnp.pad(gr, ((0, 0), (0, 0), (0, S_HPAD - N_KV), (0, LANES - 3))).reshape(nb, S_ROWS, LANES)
    n_selw = 2 * LANES
    o_s, win_new = _nsa_sample(
        page_table, cache_l.reshape(cache_l.shape[0], psz, -1).transpose(0, 2, 1),
        win_buf.reshape(nb, win_buf.shape[1], -1),
        qr.astype(F32), paged_s.reshape(nb, 1, -1), win_s.reshape(nb, 1, -1), gr,
        bd, w_cmp, pe_col, _overlap_matrix(past_len // CMP_STRIDE, n_selw),
        _block_onehot(past_len, LANES), 4)
    o_s = o_s.reshape(nb, GROUP, S_HPAD, HEAD_DIM)[:, :, :N_KV].transpose(0, 2, 1, 3).reshape(nb, d)
    ys = _linres(xs, o_s, w_out_b, nb)
    kv_shape = (4, N_KV, HEAD_DIM)
    return (yp, ys, paged.reshape(b, t, *kv_shape), paged_s.reshape(nb, 1, *kv_shape),
            win_p.reshape(b, win_p.shape[1], 2, N_KV, HEAD_DIM),
            win_new.reshape(nb, win_new.shape[1], 2, N_KV, HEAD_DIM))


def kernel(x_prompt, x_sample, state_conv_a, state_conv_b, cache_kv, state_kv_win, page_table,
           norm_mix, norm_ffn, a_w_in, a_b_in, a_w_dw, a_b_dw, a_ln_g, a_ln_b, a_w_out, a_b_out,
           b_w_in, b_w_dw, b_w_out, c_w_in, c_q_norm, c_k_norm, c_w_cmp, c_pe_cmp, c_w_out,
           ffn_w_gate, ffn_w_up, ffn_w_down):
    b, t, d = x_prompt.shape
    nb = x_sample.shape[0]
    depth = norm_mix.shape[0]
    xp = x_prompt
    xs = x_sample.reshape(nb, d)
    conv_a_p, conv_a_s, conv_b_p, conv_b_s = [], [], [], []
    kv_p, kv_s, win_p, win_s = [], [], [], []
    for i in range(depth):
        kind, j = i % 3, i // 3
        g = norm_mix[i]
        if kind == 0:
            wa = (g[None, :], a_w_in[j].astype(BF16), a_b_in[j][None, :], a_w_dw[j], a_b_dw[j][None, :],
                  a_ln_g[j][None, :], a_ln_b[j][None, :], a_w_out[j].astype(BF16), a_b_out[j][None, :])
            xp, st_p = _mixa_prompt(xp, *wa, 256)
            xs, st_s = _mixa_sample(xs, state_conv_a[j], *wa)
            conv_a_p.append(st_p)
            conv_a_s.append(st_s)
        elif kind == 1:
            wb = (g[None, :], b_w_in[j].astype(BF16), b_w_dw[j], b_w_out[j].astype(BF16))
            xp, st_p = _mixb_prompt(xp, *wb, 256)
            xs, st_s = _mixb_sample(xs, state_conv_b[j], *wb)
            conv_b_p.append(st_p)
            conv_b_s.append(st_s)
        else:
            xp, xs, rows_p, rows_s, wst_p, wst_s = _nsa_layer(
                xp, xs, g, cache_kv[j], state_kv_win[j], page_table, c_w_in[j], c_q_norm[j],
                c_k_norm[j], c_w_cmp[j], c_pe_cmp[j], c_w_out[j])
            kv_p.append(rows_p)
            kv_s.append(rows_s)
            win_p.append(wst_p)
            win_s.append(wst_s)
        wf = (norm_ffn[i][None, :], ffn_w_gate[i].astype(BF16), ffn_w_up[i].astype(BF16),
              ffn_w_down[i].astype(BF16))
        xp = _ffn(xp.reshape(b * t, d), *wf, 512).reshape(b, t, d)
        xs = _ffn(xs, *wf, nb)
    return (xp, xs.reshape(nb, 1, d), jnp.stack(conv_a_p), jnp.stack(conv_a_s), jnp.stack(conv_b_p),
            jnp.stack(conv_b_s), jnp.stack(kv_p), jnp.stack(kv_s), jnp.stack(win_p), jnp.stack(win_s))
```

```python
import functools
import math

import jax
import jax.numpy as jnp
from jax import lax
from jax.experimental import pallas as pl
from jax.experimental.pallas import tpu as pltpu

F32 = jnp.float32
BF16 = jnp.bfloat16

EPS = 1e-6
HEAD_DIM = 64
N_KV = 4
GROUP = 4
N_HEADS = N_KV * GROUP
ROT_DIM = HEAD_DIM // 4
ROPE_THETA = 500000.0
CMP_BLOCK = 32
CMP_STRIDE = 16
SEL_BLOCK = 64
N_SELECT = 16
WINDOW = 512
FORCE_SCORE = 1e9
MASK_BIAS = -1e9
M_INIT = -1e30
LOG2E = 1.4426950408889634

LANES = 128
SUBLANES = 8
VMEM_LIMIT = 56 * 1024 * 1024


def _cparams(n_grid):
    return pltpu.CompilerParams(dimension_semantics=("arbitrary",) * n_grid,
                                vmem_limit_bytes=VMEM_LIMIT)


def _const_spec(shape):
    nd = len(shape)
    return pl.BlockSpec(shape, lambda *_: (0,) * nd, pipeline_mode=pl.Buffered(1))


def _dot(a, b):
    return jnp.dot(a, b, preferred_element_type=F32)


def _dot_t(a, b):
    return lax.dot_general(a, b, (((1,), (1,)), ((), ())), preferred_element_type=F32)


def _rms(x, g):
    return x * lax.rsqrt(jnp.mean(x * x, axis=-1, keepdims=True) + EPS) * g


def _split3(x):
    hi = x.astype(BF16)
    r = x - hi.astype(F32)
    mid = r.astype(BF16)
    lo = (r - mid.astype(F32)).astype(BF16)
    return hi, mid, lo


def _ffn_body(x_ref, g_ref, wg_ref, wu_ref, wd_ref, o_ref, *, fc):
    x = x_ref[...]
    xn = _rms(x, g_ref[...]).astype(BF16)
    d_ff = wg_ref.shape[1]
    acc = None
    for c in range(d_ff // fc):
        hg = _dot(xn, wg_ref[:, c * fc:(c + 1) * fc])
        hu = _dot(xn, wu_ref[:, c * fc:(c + 1) * fc])
        a = (hg * jax.nn.sigmoid(hg) * hu).astype(BF16)
        d = _dot(a, wd_ref[c * fc:(c + 1) * fc, :])
        acc = d if acc is None else acc + d
    o_ref[...] = x + acc


def _ffn(x, g, wg, wu, wd, tm):
    m, d = x.shape
    d_ff = wg.shape[1]
    return pl.pallas_call(
        functools.partial(_ffn_body, fc=256),
        grid=(m // tm,),
        in_specs=[pl.BlockSpec((tm, d), lambda i: (i, 0)),
                  _const_spec((1, d)), _const_spec((d, d_ff)), _const_spec((d, d_ff)),
                  _const_spec((d_ff, d))],
        out_specs=pl.BlockSpec((tm, d), lambda i: (i, 0)),
        out_shape=jax.ShapeDtypeStruct((m, d), F32),
        compiler_params=_cparams(1), name="ffn",
    )(x, g, wg, wu, wd)


def _linres_body(x_ref, o_ref, w_ref, y_ref):
    y_ref[...] = x_ref[...] + _dot(o_ref[...], w_ref[...])


def _linres(x, o, w, tm):
    m, d = x.shape
    k = o.shape[1]
    return pl.pallas_call(
        _linres_body, grid=(m // tm,),
        in_specs=[pl.BlockSpec((tm, d), lambda i: (i, 0)),
                  pl.BlockSpec((tm, k), lambda i: (i, 0)), _const_spec((k, d))],
        out_specs=pl.BlockSpec((tm, d), lambda i: (i, 0)),
        out_shape=jax.ShapeDtypeStruct((m, d), F32),
        compiler_params=_cparams(1), name="linres",
    )(x, o, w)


A_HALO = 32
A_CHUNK = 32


def _ln_swish(c, g, b):
    mu = jnp.mean(c, axis=-1, keepdims=True)
    var = jnp.mean(jnp.square(c - mu), axis=-1, keepdims=True)
    y = (c - mu) * lax.rsqrt(var + EPS) * g + b
    return y * jax.nn.sigmoid(y)


def _mixa_prompt_body(x_ref, g_ref, win_ref, bin_ref, wdw8_ref, bdw_ref, lng_ref, lnb_ref,
                      wout_ref, bout_ref, y_ref, tail_ref, ubuf, ushift, cbuf, *, width):
    i = pl.program_id(1)
    tm, d = x_ref.shape[1], x_ref.shape[2]

    @pl.when(i == 0)
    def _():
        ubuf[0:A_HALO, :] = jnp.zeros((A_HALO, d), F32)

    x = x_ref[0]
    xn = _rms(x, g_ref[...]).astype(BF16)
    h = _dot(xn, win_ref[...]) + bin_ref[...]
    u = h[:, :d] * jax.nn.sigmoid(h[:, d:])
    ubuf[A_HALO:, :] = u
    span = tm + A_HALO - SUBLANES
    for s in range(1, SUBLANES):
        ushift[s - 1] = ubuf[s:s + span, :]
    off = A_HALO - (width - 1)
    for r0 in range(0, tm, A_CHUNK):
        acc = None
        for k in range(width):
            s = (off + k) % SUBLANES
            base = r0 + off + k - s
            rows = ubuf[base:base + A_CHUNK, :] if s == 0 else ushift[s - 1, base:base + A_CHUNK, :]
            t = jnp.concatenate([wdw8_ref[k]] * (A_CHUNK // SUBLANES), axis=0) * rows
            acc = t if acc is None else acc + t
        c = _ln_swish(acc + bdw_ref[...], lng_ref[...], lnb_ref[...])
        cbuf[r0:r0 + A_CHUNK, :] = c.astype(BF16)
    y_ref[0] = x + _dot(cbuf[...], wout_ref[...]) + bout_ref[...]
    last = ubuf[tm:tm + A_HALO, :]
    ubuf[0:A_HALO, :] = last

    @pl.when(i == pl.num_programs(1) - 1)
    def _():
        tail_ref[0] = last


def _mixa_prompt(x, g, w_in, b_in, w_dw, b_dw, ln_g, ln_b, w_out, b_out, tm):
    b, t, d = x.shape
    width = w_dw.shape[0]
    y, tail = pl.pallas_call(
        functools.partial(_mixa_prompt_body, width=width),
        grid=(b, t // tm),
        in_specs=[pl.BlockSpec((1, tm, d), lambda bi, i: (bi, i, 0)),
                  _const_spec((1, d)), _const_spec((d, 2 * d)), _const_spec((1, 2 * d)),
                  _const_spec((width, SUBLANES, d)), _const_spec((1, d)), _const_spec((1, d)),
                  _const_spec((1, d)), _const_spec((d, d)), _const_spec((1, d))],
        out_specs=[pl.BlockSpec((1, tm, d), lambda bi, i: (bi, i, 0)),
                   pl.BlockSpec((1, A_HALO, d), lambda bi, i: (bi, 0, 0))],
        out_shape=[jax.ShapeDtypeStruct((b, t, d), F32),
                   jax.ShapeDtypeStruct((b, A_HALO, d), F32)],
        scratch_shapes=[pltpu.VMEM((tm + A_HALO, d), F32),
                        pltpu.VMEM((SUBLANES - 1, tm + A_HALO - SUBLANES, d), F32),
                        pltpu.VMEM((tm, d), BF16)],
        compiler_params=_cparams(2), name="mixa_prompt",
    )(x, g, w_in, b_in, jnp.broadcast_to(w_dw[:, None, :], (width, SUBLANES, d)), b_dw, ln_g, ln_b,
      w_out, b_out)
    return y, tail[:, A_HALO - (width - 1):]


def _mixa_sample_body(x_ref, hist_ref, g_ref, win_ref, bin_ref, wdw_ref, bdw_ref, lng_ref,
                      lnb_ref, wout_ref, bout_ref, y_ref, st_ref, *, width):
    d = x_ref.shape[1]
    x = x_ref[...]
    xn = _rms(x, g_ref[...]).astype(BF16)
    h = _dot(xn, win_ref[...]) + bin_ref[...]
    u = h[:, :d] * jax.nn.sigmoid(h[:, d:])
    acc = None
    for k in range(width - 1):
        row = hist_ref[:, k, :]
        t = wdw_ref[k:k + 1, :] * row
        acc = t if acc is None else acc + t
        if k >= 1:
            st_ref[:, k - 1, :] = row
    acc = acc + wdw_ref[width - 1:width, :] * u
    st_ref[:, width - 2, :] = u
    c = _ln_swish(acc + bdw_ref[...], lng_ref[...], lnb_ref[...])
    y_ref[...] = x + _dot(c.astype(BF16), wout_ref[...]) + bout_ref[...]


def _mixa_sample(x, hist, g, w_in, b_in, w_dw, b_dw, ln_g, ln_b, w_out, b_out):
    n, d = x.shape
    width = w_dw.shape[0]
    return pl.pallas_call(
        functools.partial(_mixa_sample_body, width=width),
        out_shape=[jax.ShapeDtypeStruct((n, d), F32),
                   jax.ShapeDtypeStruct((n, width - 1, d), F32)],
        compiler_params=pltpu.CompilerParams(vmem_limit_bytes=VMEM_LIMIT), name="mixa_sample",
    )(x, hist, g, w_in, b_in, w_dw, b_dw, ln_g, ln_b, w_out, b_out)


B_HALO = 8


def _mixb_prompt_body(x_ref, g_ref, win_ref, wdw_ref, wout_ref, y_ref, tail_ref, cbuf, *, width):
    i = pl.program_id(1)
    tm, d = x_ref.shape[1], x_ref.shape[2]

    @pl.when(i == 0)
    def _():
        cbuf[0:B_HALO, :] = jnp.zeros((B_HALO, d), F32)

    x = x_ref[0]
    xn = _rms(x, g_ref[...]).astype(BF16)
    h = _dot(xn, win_ref[...])
    cbuf[B_HALO:, :] = h[:, d:2 * d] * h[:, 2 * d:]
    off = B_HALO - (width - 1)
    conv = None
    for k in range(width):
        t = wdw_ref[k:k + 1, :] * cbuf[off + k:off + k + tm, :]
        conv = t if conv is None else conv + t
    z = (h[:, :d] * conv).astype(BF16)
    y_ref[0] = x + _dot(z, wout_ref[...])
    last = cbuf[tm:tm + B_HALO, :]
    cbuf[0:B_HALO, :] = last

    @pl.when(i == pl.num_programs(1) - 1)
    def _():
        tail_ref[0] = last


def _mixb_prompt(x, g, w_in, w_dw, w_out, tm):
    b, t, d = x.shape
    width = w_dw.shape[0]
    y, tail = pl.pallas_call(
        functools.partial(_mixb_prompt_body, width=width),
        grid=(b, t // tm),
        in_specs=[pl.BlockSpec((1, tm, d), lambda bi, i: (bi, i, 0)),
                  _const_spec((1, d)), _const_spec((d, 3 * d)), _const_spec((width, d)),
                  _const_spec((d, d))],
        out_specs=[pl.BlockSpec((1, tm, d), lambda bi, i: (bi, i, 0)),
                   pl.BlockSpec((1, B_HALO, d), lambda bi, i: (bi, 0, 0))],
        out_shape=[jax.ShapeDtypeStruct((b, t, d), F32),
                   jax.ShapeDtypeStruct((b, B_HALO, d), F32)],
        scratch_shapes=[pltpu.VMEM((tm + B_HALO, d), F32)],
        compiler_params=_cparams(2), name="mixb_prompt",
    )(x, g, w_in, w_dw, w_out)
    return y, tail[:, B_HALO - (width - 1):]


def _mixb_sample_body(x_ref, hist_ref, g_ref, win_ref, wdw_ref, wout_ref, y_ref, st_ref, *, width):
    d = x_ref.shape[1]
    x = x_ref[...]
    xn = _rms(x, g_ref[...]).astype(BF16)
    h = _dot(xn, win_ref[...])
    cv = h[:, d:2 * d] * h[:, 2 * d:]
    conv = None
    for k in range(width - 1):
        row = hist_ref[:, k, :]
        t = wdw_ref[k:k + 1, :] * row
        conv = t if conv is None else conv + t
        if k >= 1:
            st_ref[:, k - 1, :] = row
    conv = conv + wdw_ref[width - 1:width, :] * cv
    st_ref[:, width - 2, :] = cv
    z = (h[:, :d] * conv).astype(BF16)
    y_ref[...] = x + _dot(z, wout_ref[...])


def _mixb_sample(x, hist, g, w_in, w_dw, w_out):
    n, d = x.shape
    width = w_dw.shape[0]
    return pl.pallas_call(
        functools.partial(_mixb_sample_body, width=width),
        out_shape=[jax.ShapeDtypeStruct((n, d), F32),
                   jax.ShapeDtypeStruct((n, width - 1, d), F32)],
        compiler_params=pltpu.CompilerParams(vmem_limit_bytes=VMEM_LIMIT), name="mixb_sample",
    )(x, hist, g, w_in, w_dw, w_out)


def _rope_tables(pos):
    half = ROT_DIM // 2
    inv = jnp.exp(jnp.arange(half, dtype=F32) * (-2.0 * math.log(ROPE_THETA) / ROT_DIM))
    ang = pos.astype(F32)[:, None] * inv[None, :]
    cos, sin = jnp.cos(ang), jnp.sin(ang)
    r = jnp.arange(LANES) % HEAD_DIM
    c = jnp.where(r[None, :] < ROT_DIM, cos[:, r % half], 1.0)
    s1 = jnp.where(r[None, :] < half, -sin[:, r % half], 0.0)
    s2 = jnp.where((r[None, :] >= half) & (r[None, :] < ROT_DIM), sin[:, r % half], 0.0)
    return c, s1, s2


def _head_norm_rope(blk, gvec, c, s1, s2):
    lane = lax.broadcasted_iota(jnp.int32, (1, LANES), 1)
    first = lane < HEAD_DIM
    x2 = blk * blk
    ms_a = jnp.sum(jnp.where(first, x2, 0.0), axis=-1, keepdims=True)
    ms_b = jnp.sum(jnp.where(first, 0.0, x2), axis=-1, keepdims=True)
    ms = jnp.where(first, ms_a, ms_b) * (1.0 / HEAD_DIM)
    y = blk * lax.rsqrt(ms + EPS) * gvec
    return y * c + pltpu.roll(y, LANES - ROT_DIM // 2, 1) * s1 + pltpu.roll(y, ROT_DIM // 2, 1) * s2


GATE_ROWS = 16


def _nsa_proj_body(x_ref, g_ref, wqkv_ref, wg_ref, qn_ref, kn_ref, c_ref, s1_ref, s2_ref, *outs,
                   prompt_layout):
    if prompt_layout:
        qt_ref, paged_ref, win_ref, cmp_ref, kaug_ref, vst_ref, kw_ref, vwt_ref, gt_ref = outs
    else:
        q_ref, paged_ref, win_ref, gate_ref = outs
    x = x_ref[0]
    tm = x.shape[0]
    xn = _rms(x, g_ref[...]).astype(BF16)
    h = _dot(xn, wqkv_ref[...])
    gh = _dot(xn, wg_ref[...])
    c, s1, s2 = c_ref[...], s1_ref[...], s2_ref[...]
    nq = N_HEADS * HEAD_DIM
    kvw = N_KV * HEAD_DIM
    scale = HEAD_DIM ** -0.5 * LOG2E

    def put_heads_t(dst, pair, blk_t):
        for j in range(tm // LANES):
            dst[0, 2 * pair, j] = blk_t[:HEAD_DIM, j * LANES:(j + 1) * LANES]
            dst[0, 2 * pair + 1, j] = blk_t[HEAD_DIM:, j * LANES:(j + 1) * LANES]

    for cb in range(nq // LANES):
        blk = _head_norm_rope(h[:, cb * LANES:(cb + 1) * LANES], qn_ref[...], c, s1, s2) * scale
        if prompt_layout:
            bt = blk.T.astype(BF16)
            qt_ref[0, 2 * cb] = bt[:HEAD_DIM]
            qt_ref[0, 2 * cb + 1] = bt[HEAD_DIM:]
        else:
            q_ref[0, :, cb * LANES:(cb + 1) * LANES] = blk.astype(BF16)
    if prompt_layout:
        t_abs = pl.program_id(1) * tm + lax.broadcasted_iota(jnp.int32, (tm, 1), 0)
        lane = lax.broadcasted_iota(jnp.int32, (1, LANES), 1)
        onehot = jnp.where(t_abs // SEL_BLOCK == lane, 1.0, 0.0).astype(BF16)
    for br in range(3):
        kbase = nq + br * 2 * kvw
        for cb in range(kvw // LANES):
            kblk = _head_norm_rope(h[:, kbase + cb * LANES:kbase + (cb + 1) * LANES],
                                   kn_ref[br:br + 1, :], c, s1, s2)
            vblk = h[:, kbase + kvw + cb * LANES:kbase + kvw + (cb + 1) * LANES]
            if br < 2:
                paged_ref[0, :, br * 2 * kvw + cb * LANES:br * 2 * kvw + (cb + 1) * LANES] = kblk
                paged_ref[0, :, br * 2 * kvw + kvw + cb * LANES:
                          br * 2 * kvw + kvw + (cb + 1) * LANES] = vblk
            else:
                win_ref[0, :, cb * LANES:(cb + 1) * LANES] = kblk
                win_ref[0, :, kvw + cb * LANES:kvw + (cb + 1) * LANES] = vblk
            if not prompt_layout:
                continue
            if br == 0:
                cmp_ref[0, cb] = kblk
                cmp_ref[0, kvw // LANES + cb] = vblk
            elif br == 1:
                kaug_ref[0, 2 * cb] = jnp.concatenate([onehot, kblk.astype(BF16)], axis=1)
                kaug_ref[0, 2 * cb + 1] = jnp.concatenate(
                    [onehot, pltpu.roll(kblk, HEAD_DIM, 1).astype(BF16)], axis=1)
                put_heads_t(vst_ref, cb, vblk.T.astype(BF16))
            else:
                kb = kblk.astype(BF16)
                kw_ref[0, 2 * cb] = kb[:, :HEAD_DIM]
                kw_ref[0, 2 * cb + 1] = kb[:, HEAD_DIM:]
                put_heads_t(vwt_ref, cb, vblk.T.astype(BF16))
    if prompt_layout:
        gh_t = gh.T
        for hh in range(N_KV):
            gt_ref[0, hh] = gh_t[hh * GATE_ROWS:(hh + 1) * GATE_ROWS]
    else:
        gate_ref[0] = gh


def _nsa_proj(x, g, wqkv, wgate, qn, kn, rope, tm, prompt_layout):
    b, t, d = x.shape
    c, s1, s2 = rope
    nqkv = wqkv.shape[1]
    kvw = N_KV * HEAD_DIM
    row = lambda w, dt: (jax.ShapeDtypeStruct((b, t, w), dt),
                         pl.BlockSpec((1, tm, w), lambda bi, i: (bi, i, 0)))
    tab_spec = pl.BlockSpec((tm, LANES), lambda bi, i: (i, 0))
    if prompt_layout:
        vt = (jax.ShapeDtypeStruct((b, N_KV, t // LANES, HEAD_DIM, LANES), BF16),
              pl.BlockSpec((1, N_KV, tm // LANES, HEAD_DIM, LANES), lambda bi, i: (bi, 0, i, 0, 0)))
        outs = [(jax.ShapeDtypeStruct((b, N_HEADS, HEAD_DIM, t), BF16),
                 pl.BlockSpec((1, N_HEADS, HEAD_DIM, tm), lambda bi, i: (bi, 0, 0, i))),
                row(4 * kvw, F32), row(2 * kvw, F32),
                (jax.ShapeDtypeStruct((b, 4, t, LANES), F32),
                 pl.BlockSpec((1, 4, tm, LANES), lambda bi, i: (bi, 0, i, 0))),
                (jax.ShapeDtypeStruct((b, N_KV, t, 2 * LANES), BF16),
                 pl.BlockSpec((1, N_KV, tm, 2 * LANES), lambda bi, i: (bi, 0, i, 0))),
                vt,
                (jax.ShapeDtypeStruct((b, N_KV, t, HEAD_DIM), BF16),
                 pl.BlockSpec((1, N_KV, tm, HEAD_DIM), lambda bi, i: (bi, 0, i, 0))),
                vt,
                (jax.ShapeDtypeStruct((b, N_KV, GATE_ROWS, t), F32),
                 pl.BlockSpec((1, N_KV, GATE_ROWS, tm), lambda bi, i: (bi, 0, 0, i)))]
    else:
        outs = [row(N_HEADS * HEAD_DIM, BF16), row(4 * kvw, F32), row(2 * kvw, F32),
                row(LANES, F32)]
    return pl.pallas_call(
        functools.partial(_nsa_proj_body, prompt_layout=prompt_layout), grid=(b, t // tm),
        in_specs=[pl.BlockSpec((1, tm, d), lambda bi, i: (bi, i, 0)), _const_spec((1, d)),
                  _const_spec((d, nqkv)), _const_spec((d, LANES)), _const_spec((1, LANES)),
                  _const_spec((3, LANES)), tab_spec, tab_spec, tab_spec],
        out_specs=[o[1] for o in outs], out_shape=[o[0] for o in outs],
        compiler_params=_cparams(2), name="nsa_proj" if prompt_layout else "nsa_proj_sample",
    )(x, g, wqkv, wgate, qn, kn, c, s1, s2)


def _cmp_const(w_ref, pe_ref, kv):
    cst = jnp.sum(w_ref[kv] * pe_ref[kv], axis=0, keepdims=True)
    return jnp.concatenate([cst] * N_KV, axis=1)


SHIFT_PAD = 8


def _compress_rows(load, bd_ref, kv, n_blk, shift_buf):
    first = second = None
    for l in range(CMP_STRIDE):
        xl = jnp.concatenate([load(0, l), load(1, l)], axis=1).astype(BF16)
        ta = _dot(xl, bd_ref[kv, l])
        tb = _dot(xl, bd_ref[kv, CMP_STRIDE + l])
        first = ta if first is None else first + ta
        second = tb if second is None else second + tb
    shift_buf[0:n_blk, :] = second
    shift_buf[n_blk:n_blk + SHIFT_PAD, :] = jnp.zeros((SHIFT_PAD, shift_buf.shape[1]), F32)
    return first + shift_buf[1:n_blk + 1, :]


def _compress_body(x_ref, bd_ref, w_ref, pe_ref, kc_ref, vct_ref, shift_buf):
    kv = pl.program_id(1)
    n_blk = x_ref.shape[2] // CMP_STRIDE
    out = _compress_rows(lambda c, l: x_ref[0, c, pl.ds(l, n_blk, stride=CMP_STRIDE), :],
                         bd_ref, kv, n_blk, shift_buf) + _cmp_const(w_ref, pe_ref, kv)

    @pl.when(kv == 0)
    def _():
        for hh in range(N_KV):
            kc_ref[0, hh] = out[:, hh * HEAD_DIM:(hh + 1) * HEAD_DIM].astype(BF16)

    @pl.when(kv == 1)
    def _():
        out_t = out.T.astype(BF16)
        for hh in range(N_KV):
            vct_ref[0, hh] = out_t[hh * HEAD_DIM:(hh + 1) * HEAD_DIM]


def _compress(cmp_rows, bd, w_cmp, pe_col):
    b, _, t, _ = cmp_rows.shape
    n_blk = t // CMP_STRIDE
    return pl.pallas_call(
        _compress_body, grid=(b, 2),
        in_specs=[pl.BlockSpec((1, 2, t, LANES), lambda bi, kv: (bi, kv, 0, 0)),
                  _const_spec(bd.shape), _const_spec(w_cmp.shape), _const_spec(pe_col.shape)],
        out_specs=[pl.BlockSpec((1, N_KV, n_blk, HEAD_DIM), lambda bi, kv: (bi, 0, 0, 0)),
                   pl.BlockSpec((1, N_KV, HEAD_DIM, n_blk), lambda bi, kv: (bi, 0, 0, 0))],
        out_shape=[jax.ShapeDtypeStruct((b, N_KV, n_blk, HEAD_DIM), BF16),
                   jax.ShapeDtypeStruct((b, N_KV, HEAD_DIM, n_blk), BF16)],
        scratch_shapes=[pltpu.VMEM((n_blk + SHIFT_PAD, N_KV * HEAD_DIM), F32)],
        compiler_params=_cparams(2), name="nsa_compress",
    )(cmp_rows, bd, w_cmp, pe_col)


def _softmax(s, mask, axis):
    s = jnp.where(mask, s, -jnp.inf)
    m = jnp.max(s, axis=axis, keepdims=True)
    m = jnp.where(m == -jnp.inf, 0.0, m)
    e = jnp.exp2(s - m)
    return e / jnp.maximum(jnp.sum(e, axis=axis, keepdims=True), 1e-30)


def _importance(p_sum, ov_ref, axis):
    hi, mid, lo = _split3(p_sum)
    ov = ov_ref[...]
    if axis == 0:
        return _dot(ov, hi) + _dot(ov, mid) + _dot(ov, lo)
    return _dot(hi, ov) + _dot(mid, ov) + _dot(lo, ov)


def _select_blocks(imp, q_pos, n_sel, axis):
    w = imp.shape[axis]
    j = lax.broadcasted_iota(jnp.int32, imp.shape, axis)
    cur = q_pos // SEL_BLOCK
    forced = (j == 0) | (j == cur) | (j == cur - 1)
    v = jnp.where(forced, FORCE_SCORE, imp)
    v = jnp.where((j * SEL_BLOCK > q_pos) | (j >= n_sel), -jnp.inf, v)
    valid = v > -jnp.inf
    for _ in range(min(N_SELECT, n_sel)):
        mx = jnp.max(v, axis=axis, keepdims=True)
        idx = jnp.min(jnp.where(v == mx, j, w), axis=axis, keepdims=True)
        v = jnp.where(j == idx, -jnp.inf, v)
    return valid & (v == -jnp.inf)


ONES_ROWS = 16


def _with_ones_rows(vt):
    return jnp.concatenate([vt, jnp.ones((ONES_ROWS, vt.shape[1]), BF16)], axis=0)


def _nsa_attn_body(qt_ref, kc_ref, vct_ref, kaug_ref, vst_ref, kw_ref, vwt_ref, gt_ref, ovt_ref,
                   wmask_ref, dmask_ref, o_ref, s_a, s_b, *, tq, tk):
    t0 = pl.program_id(2) * tq
    cols = GROUP * tq
    n_cmp = kc_ref.shape[2]
    t_len = kaug_ref.shape[2]
    qt = jnp.concatenate([qt_ref[0, g] for g in range(GROUP)], axis=1)
    col_t = t0 + lax.rem(lax.broadcasted_iota(jnp.int32, (1, cols), 1), tq)

    ci = lax.broadcasted_iota(jnp.int32, (n_cmp, 1), 0)
    p_c = _softmax(_dot(kc_ref[0, 0], qt), ci * CMP_STRIDE + (CMP_BLOCK - 1) <= col_t, 0)
    o_c = _dot(vct_ref[0, 0], p_c.astype(BF16))
    p_sum = p_c[:, 0:tq]
    for g in range(1, GROUP):
        p_sum = p_sum + p_c[:, g * tq:(g + 1) * tq]

    q_pos = t0 + lax.broadcasted_iota(jnp.int32, (1, tq), 1)
    sel = _select_blocks(_importance(p_sum, ovt_ref, 0), q_pos, t_len // SEL_BLOCK, 0)
    bias = jnp.where(sel, 0.0, MASK_BIAS).astype(BF16)
    rhs = jnp.concatenate([jnp.concatenate([bias] * GROUP, axis=1), qt,
                           jnp.zeros((LANES - HEAD_DIM, cols), BF16)], axis=0)

    vt_blocks = tk // LANES

    def scores(kt, buf):
        k0 = pl.multiple_of(kt * tk, tk)
        buf[...] = _dot(kaug_ref[0, 0, pl.ds(k0, tk), :], rhs)

    def consume(buf, kt, carry, causal):
        m, acc = carry
        s = buf[...]
        if causal:
            tri = dmask_ref[(t0 - kt * tk) // tq]
            s = s + jnp.concatenate([tri] * GROUP, axis=1)
        m_new = jnp.maximum(m, jnp.max(s, axis=0, keepdims=True))
        p = jnp.exp2(s - m_new).astype(BF16)
        vt = jnp.concatenate([vst_ref[0, 0, kt * vt_blocks + j] for j in range(vt_blocks)], axis=1)
        acc = jnp.exp2(m - m_new) * acc + _dot(_with_ones_rows(vt), p)
        return m_new, acc

    def pair(j, carry):
        scores(2 * j + 1, s_b)
        carry = consume(s_a, 2 * j, carry, False)
        scores(2 * j + 2, s_a)
        return consume(s_b, 2 * j + 1, carry, False)

    def tail_even(carry):
        return consume(s_a, kt_diag, carry, True)

    def tail_odd(carry):
        scores(kt_diag, s_b)
        carry = consume(s_a, kt_diag - 1, carry, False)
        return consume(s_b, kt_diag, carry, True)

    kt_diag = t0 // tk
    carry = (jnp.full((1, cols), M_INIT, F32), jnp.zeros((HEAD_DIM + ONES_ROWS, cols), F32))
    scores(0, s_a)
    carry = lax.fori_loop(0, kt_diag // 2, pair, carry)
    _, acc_s = lax.cond(kt_diag % 2 == 0, tail_even, tail_odd, carry)
    o_s = acc_s[:HEAD_DIM] / acc_s[HEAD_DIM:HEAD_DIM + 1]

    nw = WINDOW + tq
    w0 = pl.multiple_of(jnp.maximum(t0 - WINDOW, 0), tq)
    band = wmask_ref[jnp.minimum(pl.program_id(2), wmask_ref.shape[0] - 1)]
    s_w = _dot(kw_ref[0, 0, pl.ds(w0, nw), :], qt) + jnp.concatenate([band] * GROUP, axis=1)
    e_w = jnp.exp2(s_w - jnp.max(s_w, axis=0, keepdims=True)).astype(BF16)
    vwt = jnp.concatenate([vwt_ref[0, 0, w0 // LANES + j] for j in range(nw // LANES)], axis=1)
    o_w = _dot(_with_ones_rows(vwt), e_w)
    o_w = o_w[:HEAD_DIM] / o_w[HEAD_DIM:HEAD_DIM + 1]

    gate = jax.nn.sigmoid(gt_ref[0, 0])
    outs = []
    for g in range(GROUP):
        sl = slice(g * tq, (g + 1) * tq)
        outs.append(gate[3 * g:3 * g + 1] * o_c[:, sl] + gate[3 * g + 1:3 * g + 2] * o_s[:, sl]
                    + gate[3 * g + 2:3 * g + 3] * o_w[:, sl])
    o_ref[0] = jnp.concatenate(outs, axis=0).T.astype(BF16)


def _nsa_attn(qt, kc, vct, kaug, vst, kw, vwt, gt, overlap_t, tq, tk):
    b, _, _, t = qt.shape
    n_cmp = kc.shape[2]
    assert t // SEL_BLOCK <= LANES and tq % LANES == 0 and tk % tq == 0 and t % tk == 0
    assert WINDOW % tq == 0 and t >= WINDOW + tq
    kr = jnp.arange(WINDOW + tq)[None, :, None]
    tr = jnp.arange(tq)[None, None, :]
    rel = kr - tr - jnp.minimum(jnp.arange(WINDOW // tq + 1) * tq, WINDOW)[:, None, None]
    wmask = jnp.where((rel <= 0) & (rel >= -WINDOW), 0.0, MASK_BIAS).astype(F32)
    rel = jnp.arange(tk)[None, :, None] - tr - (jnp.arange(tk // tq) * tq)[:, None, None]
    dmask = jnp.where(rel <= 0, 0.0, MASK_BIAS).astype(F32)
    per_head = lambda *shape: pl.BlockSpec((1, 1) + shape, lambda bi, h, i: (bi, h) + (0,) * len(shape))
    return pl.pallas_call(
        functools.partial(_nsa_attn_body, tq=tq, tk=tk),
        grid=(b, N_KV, t // tq),
        in_specs=[pl.BlockSpec((1, GROUP, HEAD_DIM, tq), lambda bi, h, i: (bi, h, 0, i)),
                  per_head(n_cmp, HEAD_DIM), per_head(HEAD_DIM, n_cmp), per_head(t, 2 * LANES),
                  per_head(t // LANES, HEAD_DIM, LANES), per_head(t, HEAD_DIM),
                  per_head(t // LANES, HEAD_DIM, LANES),
                  pl.BlockSpec((1, 1, GATE_ROWS, tq), lambda bi, h, i: (bi, h, 0, i)),
                  _const_spec(overlap_t.shape), _const_spec(wmask.shape), _const_spec(dmask.shape)],
        out_specs=pl.BlockSpec((1, tq, GROUP * HEAD_DIM), lambda bi, h, i: (bi, i, h)),
        out_shape=jax.ShapeDtypeStruct((b, t, N_HEADS * HEAD_DIM), BF16),
        scratch_shapes=[pltpu.VMEM((tk, GROUP * tq), F32), pltpu.VMEM((tk, GROUP * tq), F32)],
        compiler_params=_cparams(3), name="nsa_attn",
    )(qt, kc, vct, kaug, vst, kw, vwt, gt, overlap_t, wmask, dmask)


S_ROWS = 32
S_HPAD = 8


def _nsa_sample_body(pt_ref, *refs, n_pg, past_len):
    pages = refs[:n_pg]
    (win_ref, q_ref, new_ref, wnew_ref, g_ref, bd_ref, w_ref, pe_ref, ov_ref, e_ref,
     o_ref, wout_ref, kcbuf, vcbuf, kst_buf, vst_buf, shift_buf) = refs[n_pg:]
    step = pl.program_id(1)
    kvw = N_KV * HEAD_DIM
    psz = pages[0].shape[2]
    n_pages = kst_buf.shape[0]
    for i in range(n_pg):
        pg = step * n_pg + i
        r0 = pl.multiple_of(pg * psz, psz)
        for buf, f0 in ((kcbuf, 0), (vcbuf, kvw)):
            rows = pages[i][0, f0:f0 + kvw, :].T
            for c in range(2):
                buf[c, pl.ds(r0, psz), :] = rows[:, c * LANES:(c + 1) * LANES]
        kst_buf[pg] = pages[i][0, 2 * kvw:3 * kvw, :].astype(BF16)
        vst_buf[pg] = pages[i][0, 3 * kvw:4 * kvw, :].astype(BF16)

    @pl.when(step == pl.num_programs(1) - 1)
    def _():
        n_blk = past_len // CMP_STRIDE
        new = new_ref[0]
        cmp_of = lambda buf, kv: (_compress_rows(
            lambda c, l: buf[c, pl.ds(l, n_blk, stride=CMP_STRIDE), :], bd_ref, kv, n_blk, shift_buf)
            + _cmp_const(w_ref, pe_ref, kv)).astype(BF16)
        kc = cmp_of(kcbuf, 0)
        vc = cmp_of(vcbuf, 1)

        row = lax.broadcasted_iota(jnp.int32, (S_ROWS, kvw), 0)
        lane = lax.broadcasted_iota(jnp.int32, (S_ROWS, kvw), 1)
        own = lane // HEAD_DIM == lax.rem(row, S_HPAD)
        q = jnp.where(own, jnp.concatenate([q_ref[0]] * N_KV, axis=1), 0.0).astype(BF16)
        qf = q.astype(F32)

        def own_head(full):
            z = jnp.where(own, full, 0.0)
            out = z[:, 0:HEAD_DIM]
            for hh in range(1, N_KV):
                out = out + z[:, hh * HEAD_DIM:(hh + 1) * HEAD_DIM]
            return out

        def score_new(k_row):
            return jnp.sum(qf * k_row.astype(BF16).astype(F32), axis=-1, keepdims=True)

        def attend(s, mask, pv, s_new, v_row):
            if mask is not None:
                s = jnp.where(mask, s, -jnp.inf)
            m = jnp.maximum(jnp.max(s, axis=-1, keepdims=True), s_new)
            e = jnp.exp2(s - m)
            e_new = jnp.exp2(s_new - m)
            den = jnp.sum(e, axis=-1, keepdims=True) + e_new
            p = e / den
            p_new = (e_new / den).astype(BF16).astype(F32)
            return own_head(pv(p.astype(BF16)) + p_new * v_row.astype(BF16).astype(F32))

        pg_chunk = 8
        chunk_t = lambda buf, c: jnp.concatenate(
            [buf[c * pg_chunk + j] for j in range(pg_chunk)], axis=1)
        n_chunk = n_pages // pg_chunk
        ck = pg_chunk * psz

        q_pos = past_len
        ci = lax.broadcasted_iota(jnp.int32, (1, n_blk), 1)
        p_c = _softmax(_dot_t(q, kc), ci * CMP_STRIDE + (CMP_BLOCK - 1) <= q_pos, 1)
        o_c = own_head(_dot(p_c.astype(BF16), vc))
        p_sum = p_c[0:S_HPAD]
        for g in range(1, GROUP):
            p_sum = p_sum + p_c[g * S_HPAD:(g + 1) * S_HPAD]

        n_sel = past_len // SEL_BLOCK + 1
        sel = _select_blocks(_importance(p_sum, ov_ref, 1), jnp.full((S_HPAD, 1), q_pos, jnp.int32),
                             n_sel, 1)
        bias = jnp.where(sel, 0.0, MASK_BIAS).astype(BF16)
        bias = jnp.concatenate([bias] * GROUP, axis=0)

        s_s = (jnp.concatenate([_dot(q, chunk_t(kst_buf, c)) for c in range(n_chunk)], axis=1)
               + _dot_t(bias[:, 0:e_ref.shape[1]], e_ref[...]))

        def pv_sel(p):
            acc = None
            for c in range(n_chunk):
                t = _dot_t(p[:, c * ck:(c + 1) * ck], chunk_t(vst_buf, c))
                acc = t if acc is None else acc + t
            return acc

        o_s = attend(s_s, s_s > 0.5 * MASK_BIAS, pv_sel,
                     score_new(new[:, 2 * kvw:3 * kvw]), new[:, 3 * kvw:4 * kvw])

        wnew = wnew_ref[0]
        kw = win_ref[0, :, 0:kvw].astype(BF16)
        vw = win_ref[0, :, kvw:2 * kvw].astype(BF16)
        o_w = attend(_dot_t(q, kw), None, lambda p: _dot(p, vw),
                     score_new(wnew[:, 0:kvw]), wnew[:, kvw:2 * kvw])

        gate = jax.nn.sigmoid(g_ref[0])
        o_ref[0] = (gate[:, 0:1] * o_c + gate[:, 1:2] * o_s + gate[:, 2:3] * o_w).astype(BF16)

        wb = win_ref.shape[1]
        wout_ref[0, 0:wb - 1, :] = win_ref[0, 1:wb, :]
        wout_ref[0, wb - 1:wb, :] = wnew


def _nsa_sample(page_table, cache_t, win_buf, q_rows, new_rows, wnew_rows, gate_rows,
                bd, w_cmp, pe_col, overlap, onehot, n_pg):
    nb, n_pages = page_table.shape
    _, roww, psz = cache_t.shape
    past_len = n_pages * psz
    kvw = N_KV * HEAD_DIM
    wb = win_buf.shape[1]
    assert psz == LANES and n_pages % n_pg == 0 and n_pages % 8 == 0

    def page_spec(i):
        return pl.BlockSpec((1, roww, psz), lambda b, s, pt: (pt[b, s * n_pg + i], 0, 0))

    per_b = lambda shape: pl.BlockSpec((1,) + shape, lambda b, s, pt: (b, 0, 0))
    cst = lambda a: pl.BlockSpec(a.shape, lambda b, s, pt: (0,) * a.ndim, pipeline_mode=pl.Buffered(1))
    grid_spec = pltpu.PrefetchScalarGridSpec(
        num_scalar_prefetch=1, grid=(nb, n_pages // n_pg),
        in_specs=[page_spec(i) for i in range(n_pg)] + [
            per_b((wb, 2 * kvw)), per_b((S_ROWS, HEAD_DIM)), per_b((1, roww)), per_b((1, 2 * kvw)),
            per_b((S_ROWS, LANES)), cst(bd), cst(w_cmp), cst(pe_col), cst(overlap), cst(onehot)],
        out_specs=[per_b((S_ROWS, HEAD_DIM)), per_b((wb, 2 * kvw))],
        scratch_shapes=[pltpu.VMEM((2, past_len, LANES), F32), pltpu.VMEM((2, past_len, LANES), F32),
                        pltpu.VMEM((n_pages, kvw, psz), BF16), pltpu.VMEM((n_pages, kvw, psz), BF16),
                        pltpu.VMEM((past_len // CMP_STRIDE + SHIFT_PAD, kvw), F32)])
    return pl.pallas_call(
        functools.partial(_nsa_sample_body, n_pg=n_pg, past_len=past_len),
        grid_spec=grid_spec,
        out_shape=[jax.ShapeDtypeStruct((nb, S_ROWS, HEAD_DIM), BF16),
                   jax.ShapeDtypeStruct((nb, wb, 2 * kvw), F32)],
        compiler_params=_cparams(2), name="nsa_sample",
    )(page_table, *([cache_t] * n_pg), win_buf, q_rows, new_rows, wnew_rows, gate_rows,
      bd, w_cmp, pe_col, overlap, onehot)


def _overlap_matrix(n_cmp, width):
    ci = jnp.arange(n_cmp)[:, None] * CMP_STRIDE
    start = jnp.arange(width)[None, :] * SEL_BLOCK
    return ((ci < start + SEL_BLOCK) & (ci + CMP_BLOCK > start)).astype(BF16)


def _block_onehot(t, width):
    return (jnp.arange(t)[:, None] // SEL_BLOCK == jnp.arange(width)[None, :]).astype(BF16)


def _nsa_weights(w_in, q_norm, k_norm, w_cmp, pe_cmp):
    nq = N_HEADS * HEAD_DIM
    nkv = 6 * N_KV * HEAD_DIM
    d = w_in.shape[0]
    wqkv = w_in[:, :nq + nkv].astype(BF16)
    wg = w_in[:, nq + nkv:].reshape(d, N_KV, GROUP * 3)
    wg = jnp.pad(wg, ((0, 0), (0, 0), (0, GATE_ROWS - GROUP * 3))).reshape(d, N_KV * GATE_ROWS)
    wg = jnp.pad(wg, ((0, 0), (0, LANES - N_KV * GATE_ROWS))).astype(BF16)
    qn = jnp.tile(q_norm, LANES // HEAD_DIM)[None, :]
    kn = jnp.tile(k_norm, (1, LANES // HEAD_DIM))
    w4 = w_cmp.reshape(2, CMP_BLOCK, HEAD_DIM, HEAD_DIM)
    eye = jnp.eye(N_KV, dtype=w_cmp.dtype)
    bd = (eye[None, None, :, None, :, None] * w4[:, :, None, :, None, :]).reshape(
        2, CMP_BLOCK, N_KV * HEAD_DIM, N_KV * HEAD_DIM).astype(BF16)
    pe_col = pe_cmp.reshape(2, CMP_BLOCK * HEAD_DIM, 1)
    return wqkv, wg, qn, kn, bd, pe_col


def _nsa_layer(xp, xs, g, cache_l, win_buf, page_table, w_in, q_norm, k_norm, w_cmp, pe_cmp, w_out):
    b, t, d = xp.shape
    nb = xs.shape[0]
    wqkv, wg, qn, kn, bd, pe_col = _nsa_weights(w_in, q_norm, k_norm, w_cmp, pe_cmp)
    w_out_b = w_out.astype(BF16)
    g2 = g[None, :]

    rope_p = _rope_tables(jnp.arange(t, dtype=jnp.int32))
    qt, paged, win, cmp_rows, kaug, vst, kw, vwt, gt = _nsa_proj(
        xp, g2, wqkv, wg, qn, kn, rope_p, 256, True)
    kc, vct = _compress(cmp_rows, bd, w_cmp, pe_col)
    o = _nsa_attn(qt, kc, vct, kaug, vst, kw, vwt, gt,
                  _overlap_matrix(t // CMP_STRIDE, LANES).T, 512, 512)
    yp = _linres(xp.reshape(b * t, d), o.reshape(b * t, d), w_out_b, 512).reshape(b, t, d)
    win_p = win[:, t - min(WINDOW, t):]

    n_pages = page_table.shape[1]
    psz = cache_l.shape[1]
    past_len = n_pages * psz
    rope_s = _rope_tables(jnp.full((nb,), past_len, jnp.int32))
    q_s, paged_s, win_s, gates_s = _nsa_proj(xs[None], g2, wqkv, wg, qn, kn, rope_s, nb, False)
    qr = q_s.reshape(nb, N_KV, GROUP, HEAD_DIM).transpose(0, 2, 1, 3)
    qr = jnp.pad(qr, ((0, 0), (0, 0), (0, S_HPAD - N_KV), (0, 0))).reshape(nb, S_ROWS, HEAD_DIM)
    gr = gates_s.reshape(nb, LANES)[:, :N_KV * GATE_ROWS].reshape(nb, N_KV, GATE_ROWS)
    gr = gr[:, :, :GROUP * 3].reshape(nb, N_KV, GROUP, 3)
    gr = gr.transpose(0, 2, 1, 3)
    gr = jnp.pad(gr, ((0, 0), (0, 0), (0, S_HPAD - N_KV), (0, LANES - 3))).reshape(nb, S_ROWS, LANES)
    n_selw = 2 * LANES
    o_s, win_new = _nsa_sample(
        page_table, cache_l.reshape(cache_l.shape[0], psz, -1).transpose(0, 2, 1),
        win_buf.reshape(nb, win_buf.shape[1], -1),
        qr.astype(F32), paged_s.reshape(nb, 1, -1), win_s.reshape(nb, 1, -1), gr,
        bd, w_cmp, pe_col, _overlap_matrix(past_len // CMP_STRIDE, n_selw),
        _block_onehot(past_len, LANES), 8)
    o_s = o_s.reshape(nb, GROUP, S_HPAD, HEAD_DIM)[:, :, :N_KV].transpose(0, 2, 1, 3).reshape(nb, d)
    ys = _linres(xs, o_s, w_out_b, nb)
    kv_shape = (4, N_KV, HEAD_DIM)
    return (yp, ys, paged.reshape(b, t, *kv_shape), paged_s.reshape(nb, 1, *kv_shape),
            win_p.reshape(b, win_p.shape[1], 2, N_KV, HEAD_DIM),
            win_new.reshape(nb, win_new.shape[1], 2, N_KV, HEAD_DIM))


def kernel(x_prompt, x_sample, state_conv_a, state_conv_b, cache_kv, state_kv_win, page_table,
           norm_mix, norm_ffn, a_w_in, a_b_in, a_w_dw, a_b_dw, a_ln_g, a_ln_b, a_w_out, a_b_out,
           b_w_in, b_w_dw, b_w_out, c_w_in, c_q_norm, c_k_norm, c_w_cmp, c_pe_cmp, c_w_out,
           ffn_w_gate, ffn_w_up, ffn_w_down):
    b, t, d = x_prompt.shape
    nb = x_sample.shape[0]
    depth = norm_mix.shape[0]
    xp = x_prompt
    xs = x_sample.reshape(nb, d)
    conv_a_p, conv_a_s, conv_b_p, conv_b_s = [], [], [], []
    kv_p, kv_s, win_p, win_s = [], [], [], []
    for i in range(depth):
        kind, j = i % 3, i // 3
        g = norm_mix[i]
        if kind == 0:
            wa = (g[None, :], a_w_in[j].astype(BF16), a_b_in[j][None, :], a_w_dw[j], a_b_dw[j][None, :],
                  a_ln_g[j][None, :], a_ln_b[j][None, :], a_w_out[j].astype(BF16), a_b_out[j][None, :])
            xp, st_p = _mixa_prompt(xp, *wa, 256)
            xs, st_s = _mixa_sample(xs, state_conv_a[j], *wa)
            conv_a_p.append(st_p)
            conv_a_s.append(st_s)
        elif kind == 1:
            wb = (g[None, :], b_w_in[j].astype(BF16), b_w_dw[j], b_w_out[j].astype(BF16))
            xp, st_p = _mixb_prompt(xp, *wb, 256)
            xs, st_s = _mixb_sample(xs, state_conv_b[j], *wb)
            conv_b_p.append(st_p)
            conv_b_s.append(st_s)
        else:
            xp, xs, rows_p, rows_s, wst_p, wst_s = _nsa_layer(
                xp, xs, g, cache_kv[j], state_kv_win[j], page_table, c_w_in[j], c_q_norm[j],
                c_k_norm[j], c_w_cmp[j], c_pe_cmp[j], c_w_out[j])
            kv_p.append(rows_p)
            kv_s.append(rows_s)
            win_p.append(wst_p)
            win_s.append(wst_s)
        wf = (norm_ffn[i][None, :], ffn_w_gate[i].astype(BF16), ffn_w_up[i].astype(BF16),
              ffn_w_down[i].astype(BF16))
        xp = _ffn(xp.reshape(b * t, d), *wf, 512).reshape(b, t, d)
        xs = _ffn(xs, *wf, nb)
    return (xp, xs.reshape(nb, 1, d), jnp.stack(conv_a_p), jnp.stack(conv_a_s), jnp.stack(conv_b_p),
            jnp.stack(conv_b_s), jnp.stack(kv_p), jnp.stack(kv_s), jnp.stack(win_p), jnp.stack(win_s))
```
